```python
import math
import jax, jax.numpy as jnp
from jax import lax
import numpy as np

D_MODEL = 1024
BATCH = 8
SEQ = 4096
DEPTH = 2

N_META = 16
BLOCK = 128
LEAD = BLOCK
N_PAD = LEAD - N_META
ATTN_HEADS = 8
ATTN_KV_HEADS = 2
HEAD_DIM = 64
WINDOW = BLOCK
ATTN_GROUP = ATTN_HEADS // ATTN_KV_HEADS
ATTN_WIDTH = ATTN_HEADS * HEAD_DIM
KV_WIDTH = ATTN_KV_HEADS * HEAD_DIM
CONV_WIDTH = 512
CONV_KERNEL = 31
MLSTM_HEADS = 4
MLSTM_WIDTH = 512
MLSTM_HEAD_DIM = MLSTM_WIDTH // MLSTM_HEADS
MLSTM_CHUNK = 64
QK_CONV_KERNEL = 4
D_FF = 2816
N_BRANCH = 3
SPLIT_SIZES = (ATTN_WIDTH, KV_WIDTH, KV_WIDTH,
               CONV_WIDTH, CONV_WIDTH,
               MLSTM_WIDTH, MLSTM_WIDTH, MLSTM_WIDTH, MLSTM_WIDTH,
               MLSTM_HEADS, MLSTM_HEADS,
               N_BRANCH * D_MODEL)
N_IN = sum(SPLIT_SIZES)
RMS_EPS = 1e-6
LN_EPS = 1e-5
NEG = -1e30

kernel_name = "hybrid_swa_conformer_mlstm_macaron"


def rms_norm(x, g):
    x32 = x.astype(jnp.float32)
    y = x32 * lax.rsqrt(jnp.mean(x32 * x32, axis=-1, keepdims=True) + RMS_EPS)
    return (y * g.astype(jnp.float32)).astype(x.dtype)


def layer_norm(x, g, b):
    x32 = x.astype(jnp.float32)
    mu = jnp.mean(x32, axis=-1, keepdims=True)
    xc = x32 - mu
    y = xc * lax.rsqrt(jnp.mean(xc * xc, axis=-1, keepdims=True) + LN_EPS)
    return (y * g.astype(jnp.float32) + b.astype(jnp.float32)).astype(x.dtype)


def causal_depthwise_conv(x, w, b, valid):
    x = jnp.where(valid[None, :, None], x, jnp.zeros((), x.dtype))
    ksz = w.shape[0]
    y = lax.conv_general_dilated(
        x, w[:, None, :].astype(x.dtype), window_strides=(1,),
        padding=[(ksz - 1, 0)], dimension_numbers=("NWC", "WIO", "NWC"),
        feature_group_count=x.shape[-1])
    return y + b.astype(x.dtype)


def swiglu_ffn(x, g, w1, w3, w2):
    h = rms_norm(x, g)
    return (jax.nn.silu(h @ w1) * (h @ w3)) @ w2


def sliding_window_attention(q, k, v, q_gain, k_gain, sinks, valid):
    bsz, t_len = q.shape[:2]
    nb = t_len // BLOCK
    q = rms_norm(q, q_gain).astype(jnp.float32) * (HEAD_DIM ** -0.5)
    k = rms_norm(k, k_gain).astype(jnp.float32)
    v32 = v.astype(jnp.float32)
    qb = q.reshape(bsz, nb, BLOCK, ATTN_KV_HEADS, ATTN_GROUP, HEAD_DIM)

    def with_prev(t):
        tb = t.reshape(bsz, nb, BLOCK, ATTN_KV_HEADS, HEAD_DIM)
        prev = jnp.pad(tb[:, :-1], [(0, 0), (1, 0), (0, 0), (0, 0), (0, 0)])
        return jnp.concatenate([prev, tb], axis=2)

    kb, vb = with_prev(k), with_prev(v32)
    vblk = valid.reshape(nb, BLOCK)
    kvalid = jnp.concatenate([jnp.pad(vblk[:-1], [(1, 0), (0, 0)]), vblk], axis=1)
    qpos = jnp.arange(BLOCK)[:, None] + BLOCK
    kpos = jnp.arange(2 * BLOCK)[None, :]
    band = (kpos <= qpos) & (qpos - kpos < WINDOW)
    mask = band[None] & kvalid[:, None, :]
    s = jnp.einsum("bnqhgd,bnkhd->bnhgqk", qb, kb)
    s = jnp.where(mask[None, :, None, None], s, NEG)
    sink = jnp.broadcast_to(
        sinks.astype(jnp.float32).reshape(ATTN_KV_HEADS, ATTN_GROUP)[None, None, :, :, None, None],
        s.shape[:-1] + (1,))
    p = jax.nn.softmax(jnp.concatenate([s, sink], axis=-1), axis=-1)[..., :-1]
    o = jnp.einsum("bnhgqk,bnkhd->bnqhgd", p, vb)
    return o.reshape(bsz, t_len, ATTN_WIDTH).astype(q_gain.dtype)


def mlstm_chunkwise(q, k, v, log_i, log_f):
    bsz, t_len, nh, d = q.shape
    L = MLSTM_CHUNK
    nc = t_len // L

    def chunk(t):
        return t.reshape(bsz, nc, L, nh, d).transpose(0, 3, 1, 2, 4)

    qc = chunk(q) * (d ** -0.5)
    kc, vc = chunk(k), chunk(v)
    li = log_i.reshape(bsz, nc, L, nh).transpose(0, 3, 1, 2)
    lf = log_f.reshape(bsz, nc, L, nh).transpose(0, 3, 1, 2)
    bcum = jnp.cumsum(lf, axis=-1)
    b_end = bcum[..., -1]
    causal = jnp.tril(jnp.ones((L, L), dtype=bool))
    log_d = jnp.where(causal, bcum[..., :, None] - bcum[..., None, :] + li[..., None, :], NEG)

    log_e = b_end[..., None] - bcum + li
    m_loc = jnp.max(log_e, axis=-1)
    w_e = jnp.exp(log_e - m_loc[..., None])
    c_loc = jnp.einsum("bhcs,bhcsd,bhcse->bhcde", w_e, vc, kc)
    n_loc = jnp.einsum("bhcs,bhcse->bhce", w_e, kc)

    def step(carry, xs):
        c_st, n_st, m_st = carry
        c_l, n_l, m_l, b_l = xs
        m_new = jnp.maximum(b_l + m_st, m_l)
        a = jnp.exp(b_l + m_st - m_new)
        cc = jnp.exp(m_l - m_new)
        c_new = a[..., None, None] * c_st + cc[..., None, None] * c_l
        n_new = a[..., None] * n_st + cc[..., None] * n_l
        return (c_new, n_new, m_new), (c_st, n_st, m_st)

    init = (jnp.zeros((bsz, nh, d, d), jnp.float32), jnp.zeros((bsz, nh, d), jnp.float32),
            jnp.full((bsz, nh), NEG, jnp.float32))
    xs = (jnp.moveaxis(c_loc, 2, 0), jnp.moveaxis(n_loc, 2, 0),
          jnp.moveaxis(m_loc, 2, 0), jnp.moveaxis(b_end, 2, 0))
    _, (c_prev, n_prev, m_prev) = lax.scan(step, init, xs)
    c_prev = jnp.moveaxis(c_prev, 0, 2)
    n_prev = jnp.moveaxis(n_prev, 0, 2)
    m_prev = jnp.moveaxis(m_prev, 0, 2)

    m_inter = bcum + m_prev[..., None]
    m_t = jnp.maximum(m_inter, jnp.max(log_d, axis=-1))
    w_inter = jnp.exp(m_inter - m_t)
    s = jnp.einsum("bhctd,bhcsd->bhcts", qc, kc) * jnp.exp(log_d - m_t[..., None])
    num = jnp.einsum("bhcts,bhcsd->bhctd", s, vc) + \
        w_inter[..., None] * jnp.einsum("bhcde,bhcte->bhctd", c_prev, qc)
    den = jnp.sum(s, axis=-1) + w_inter * jnp.einsum("bhce,bhcte->bhct", n_prev, qc)
    h = num / jnp.maximum(jnp.abs(den), jnp.exp(-m_t))[..., None]
    return h.transpose(0, 2, 3, 1, 4).reshape(bsz, t_len, nh * d)


def hybrid_mixer(x, valid, mix_norm, w_in, gate_bias, attn_q_norm, attn_k_norm, attn_sinks,
                 w_o_attn, conv_dw_w, conv_dw_b, conv_ln_g, conv_ln_b, w_o_conv,
                 mlstm_qk_conv_w, mlstm_qk_conv_b, mlstm_igate_bias, mlstm_fgate_bias,
                 w_o_mlstm, w_out):
    bsz, t_len, _ = x.shape
    h = rms_norm(x, mix_norm)
    z = h @ w_in
    parts = []
    off = 0
    for n in SPLIT_SIZES:
        parts.append(z[..., off:off + n])
        off += n
    aq, ak, av, ca, cg, mq, mk, mv, mo, mi, mf, zg = parts

    ya = sliding_window_attention(
        aq.reshape(bsz, t_len, ATTN_HEADS, HEAD_DIM),
        ak.reshape(bsz, t_len, ATTN_KV_HEADS, HEAD_DIM),
        av.reshape(bsz, t_len, ATTN_KV_HEADS, HEAD_DIM),
        attn_q_norm, attn_k_norm, attn_sinks, valid).astype(x.dtype) @ w_o_attn

    u = ca * jax.nn.sigmoid(cg)
    u = causal_depthwise_conv(u, conv_dw_w, conv_dw_b, valid)
    u = jax.nn.silu(layer_norm(u, conv_ln_g, conv_ln_b))
    yc = u @ w_o_conv

    qk = jax.nn.silu(causal_depthwise_conv(jnp.concatenate([mq, mk], axis=-1),
                                           mlstm_qk_conv_w, mlstm_qk_conv_b, valid))
    qm, km = qk[..., :MLSTM_WIDTH], qk[..., MLSTM_WIDTH:]
    log_i = mi.astype(jnp.float32) + mlstm_igate_bias.astype(jnp.float32)
    log_i = jnp.where(valid[None, :, None], log_i, NEG)
    log_f = jax.nn.log_sigmoid(mf.astype(jnp.float32) + mlstm_fgate_bias.astype(jnp.float32))
    shp = (bsz, t_len, MLSTM_HEADS, MLSTM_HEAD_DIM)
    h_tilde = mlstm_chunkwise(qm.reshape(shp).astype(jnp.float32),
                              km.reshape(shp).astype(jnp.float32),
                              mv.reshape(shp).astype(jnp.float32), log_i, log_f)
    ym = (jax.nn.sigmoid(mo) * h_tilde.astype(x.dtype)) @ w_o_mlstm

    g = jax.nn.sigmoid(zg.reshape(bsz, t_len, N_BRANCH, D_MODEL) + gate_bias)
    y = g[:, :, 0] * ya + g[:, :, 1] * yc + g[:, :, 2] * ym
    return y @ w_out


def setup_inputs(seed: int = 0) -> dict:
    key = jax.random.key(seed)
    ks = jax.random.split(key, 32)
    f32 = jnp.float32

    def nrm(k, shape, scale):
        return jax.random.normal(k, shape, f32) * scale

    def gain(k, shape):
        return 1.0 + 0.05 * jax.random.normal(k, shape, f32)

    L = DEPTH
    return {
        "x": jax.random.normal(ks[0], (BATCH, SEQ, D_MODEL), f32),
        "meta_tokens": nrm(ks[1], (N_META, D_MODEL), 1.0),
        "ffn1_norm": gain(ks[2], (L, D_MODEL)),
        "ffn1_w1": nrm(ks[3], (L, D_MODEL, D_FF), D_MODEL ** -0.5),
        "ffn1_w3": nrm(ks[4], (L, D_MODEL, D_FF), D_MODEL ** -0.5),
        "ffn1_w2": nrm(ks[5], (L, D_FF, D_MODEL), D_FF ** -0.5),
        "mix_norm": gain(ks[6], (L, D_MODEL)),
        "w_in": nrm(ks[7], (L, D_MODEL, N_IN), D_MODEL ** -0.5),
        "gate_bias": nrm(ks[8], (L, N_BRANCH, D_MODEL), 0.1),
        "attn_q_norm": gain(ks[9], (L, HEAD_DIM)),
        "attn_k_norm": gain(ks[10], (L, HEAD_DIM)),
        "attn_sinks": nrm(ks[11], (L, ATTN_HEADS), 0.5),
        "w_o_attn": nrm(ks[12], (L, ATTN_WIDTH, D_MODEL), ATTN_WIDTH ** -0.5),
        "conv_dw_w": nrm(ks[13], (L, CONV_KERNEL, CONV_WIDTH), CONV_KERNEL ** -0.5),
        "conv_dw_b": nrm(ks[14], (L, CONV_WIDTH), 0.02),
        "conv_ln_g": gain(ks[15], (L, CONV_WIDTH)),
        "conv_ln_b": nrm(ks[16], (L, CONV_WIDTH), 0.02),
        "w_o_conv": nrm(ks[17], (L, CONV_WIDTH, D_MODEL), CONV_WIDTH ** -0.5),
        "mlstm_qk_conv_w": nrm(ks[18], (L, QK_CONV_KERNEL, 2 * MLSTM_WIDTH), QK_CONV_KERNEL ** -0.5),
        "mlstm_qk_conv_b": nrm(ks[19], (L, 2 * MLSTM_WIDTH), 0.02),
        "mlstm_igate_bias": nrm(ks[20], (L, MLSTM_HEADS), 0.1),
        "mlstm_fgate_bias": 3.0 + nrm(ks[21], (L, MLSTM_HEADS), 0.5),
        "w_o_mlstm": nrm(ks[22], (L, MLSTM_WIDTH, D_MODEL), MLSTM_WIDTH ** -0.5),
        "w_out": nrm(ks[23], (L, D_MODEL, D_MODEL), D_MODEL ** -0.5),
        "ffn2_norm": gain(ks[24], (L, D_MODEL)),
        "ffn2_w1": nrm(ks[25], (L, D_MODEL, D_FF), D_MODEL ** -0.5),
        "ffn2_w3": nrm(ks[26], (L, D_MODEL, D_FF), D_MODEL ** -0.5),
        "ffn2_w2": nrm(ks[27], (L, D_FF, D_MODEL), D_FF ** -0.5),
    }


def reference(x, meta_tokens, ffn1_norm, ffn1_w1, ffn1_w3, ffn1_w2, mix_norm, w_in, gate_bias,
              attn_q_norm, attn_k_norm, attn_sinks, w_o_attn, conv_dw_w, conv_dw_b, conv_ln_g,
              conv_ln_b, w_o_conv, mlstm_qk_conv_w, mlstm_qk_conv_b, mlstm_igate_bias,
              mlstm_fgate_bias, w_o_mlstm, w_out, ffn2_norm, ffn2_w1, ffn2_w3, ffn2_w2):
    bsz = x.shape[0]
    lead = jnp.concatenate([
        jnp.zeros((bsz, N_PAD, D_MODEL), x.dtype),
        jnp.broadcast_to(meta_tokens.astype(x.dtype)[None], (bsz, N_META, D_MODEL))], axis=1)
    h = jnp.concatenate([lead, x], axis=1)
    valid = jnp.arange(h.shape[1]) >= N_PAD
    for l in range(DEPTH):
        h = h + 0.5 * swiglu_ffn(h, ffn1_norm[l], ffn1_w1[l], ffn1_w3[l], ffn1_w2[l])
        h = h + hybrid_mixer(h, valid, mix_norm[l], w_in[l], gate_bias[l], attn_q_norm[l],
                             attn_k_norm[l], attn_sinks[l], w_o_attn[l], conv_dw_w[l],
                             conv_dw_b[l], conv_ln_g[l], conv_ln_b[l], w_o_conv[l],
                             mlstm_qk_conv_w[l], mlstm_qk_conv_b[l], mlstm_igate_bias[l],
                             mlstm_fgate_bias[l], w_o_mlstm[l], w_out[l])
        h = h + 0.5 * swiglu_ffn(h, ffn2_norm[l], ffn2_w1[l], ffn2_w3[l], ffn2_w2[l])
    return h[:, LEAD:]
```

```python
import functools
import math

import jax
import jax.numpy as jnp
from jax import lax
from jax.experimental import pallas as pl
from jax.experimental.pallas import tpu as pltpu

D_MODEL = 1024
N_META = 16
BLOCK = 128
LEAD = BLOCK
N_PAD = LEAD - N_META
ATTN_HEADS = 8
ATTN_KV_HEADS = 2
HEAD_DIM = 64
ATTN_GROUP = ATTN_HEADS // ATTN_KV_HEADS
ATTN_WIDTH = ATTN_HEADS * HEAD_DIM
KV_WIDTH = ATTN_KV_HEADS * HEAD_DIM
CONV_WIDTH = 512
CONV_KERNEL = 31
MLSTM_HEADS = 4
MLSTM_WIDTH = 512
MLSTM_HEAD_DIM = MLSTM_WIDTH // MLSTM_HEADS
MLSTM_CHUNK = 64
QK_CONV_KERNEL = 4
D_FF = 2816
N_BRANCH = 3
RMS_EPS = 1e-6
LN_EPS = 1e-5
NEG = -1e30

V7X_LANES = 128
V7X_SUBLANES = 8
V7X_VMEM_BYTES = 64 * 1024 * 1024

OFF_AQ = 0
OFF_AK = OFF_AQ + ATTN_WIDTH
OFF_AV = OFF_AK + KV_WIDTH
OFF_CA = OFF_AV + KV_WIDTH
OFF_CG = OFF_CA + CONV_WIDTH
OFF_MQ = OFF_CG + CONV_WIDTH
OFF_MK = OFF_MQ + MLSTM_WIDTH
OFF_MV = OFF_MK + MLSTM_WIDTH
OFF_MO = OFF_MV + MLSTM_WIDTH
OFF_ZG = OFF_MO + MLSTM_WIDTH
OFF_IF = OFF_ZG + N_BRANCH * D_MODEL
N_IN_PACKED = OFF_IF + V7X_LANES
ORIG_OFF_IF = OFF_ZG
ORIG_OFF_ZG = ORIG_OFF_IF + 2 * MLSTM_HEADS

CONV_TAIL = 32
QK_TAIL = 8

FFN_ROW_TILE = 512
MIXER_TIME_TILE = 384
FFN_VMEM_LIMIT = 56 * 1024 * 1024
MIXER_VMEM_LIMIT = 60 * 1024 * 1024

F32 = jnp.float32
BF16 = jnp.bfloat16


def _const_spec(shape):
    nd = len(shape)
    return pl.BlockSpec(shape, lambda *_: (0,) * nd, pipeline_mode=pl.Buffered(1))


def _dot(a, b):
    return jnp.dot(a, b, preferred_element_type=F32)


def _dot_nt(a, b):
    return lax.dot_general(a, b, (((1,), (1,)), ((), ())), preferred_element_type=F32)


def _dot_tn(a, b):
    return lax.dot_general(a, b, (((0,), (0,)), ((), ())), preferred_element_type=F32)


def _rms_rows(x, gain):
    ms = jnp.mean(x * x, axis=-1, keepdims=True)
    return x * lax.rsqrt(ms + RMS_EPS) * gain


def _seg_sumsq(x, ones_ref):
    x2 = x * x
    hi = x2.astype(BF16)
    lo = (x2 - hi.astype(F32)).astype(BF16)
    ones = ones_ref[...]
    return _dot(hi, ones) + _dot(lo, ones)


def _ffn_body(x_ref, g_ref, w1_ref, w3_ref, w2_ref, o_ref):
    x = x_ref[...]
    h = _rms_rows(x, g_ref[...]).astype(BF16)
    a = _dot(h, w1_ref[...])
    b = _dot(h, w3_ref[...])
    u = (a * jax.nn.sigmoid(a) * b).astype(BF16)
    o_ref[...] = x + 0.5 * _dot(u, w2_ref[...])


def _ffn(x2d, g, w1, w3, w2):
    rows = x2d.shape[0]
    tm = FFN_ROW_TILE
    assert rows % tm == 0
    return pl.pallas_call(
        _ffn_body,
        grid=(rows // tm,),
        in_specs=[
            pl.BlockSpec((tm, D_MODEL), lambda i: (i, 0)),
            _const_spec((1, D_MODEL)),
            _const_spec((D_MODEL, D_FF)),
            _const_spec((D_MODEL, D_FF)),
            _const_spec((D_FF, D_MODEL)),
        ],
        out_specs=pl.BlockSpec((tm, D_MODEL), lambda i: (i, 0)),
        out_shape=jax.ShapeDtypeStruct(x2d.shape, F32),
        compiler_params=pltpu.CompilerParams(
            dimension_semantics=("arbitrary",), vmem_limit_bytes=FFN_VMEM_LIMIT),
        name="ffn",
    )(x2d, g, w1, w3, w2)


def _mixer_body(sinks_ref, x_ref, g_ref, win_ref, gb_ref, qg_ref, kg_ref, woa_ref,
                cw_ref, cb_ref, lng_ref, lnb_ref, woc_ref, qkw_ref, qkb_ref, ifb_ref,
                wom_ref, wout_ref, onesq_ref, onesk_ref, band_ref,
                o_ref,
                kext, vext, kbias, uext, qkext, qm_s, km_s, vm_s, ht_s, bcum_s, r_s, ya_s,
                c_s, n_s, m_s, *, tt, n_pad):
    t_idx = pl.program_id(1)
    nblk = tt // BLOCK
    nchunk = tt // MLSTM_CHUNK

    @pl.when(t_idx == 0)
    def _init_state():
        kext[0:BLOCK, :] = jnp.zeros((BLOCK, KV_WIDTH), BF16)
        vext[0:BLOCK, :] = jnp.zeros((BLOCK, KV_WIDTH), BF16)
        kbias[:, 0:BLOCK] = jnp.full((1, BLOCK), NEG, F32)
        uext[0:CONV_TAIL, :] = jnp.zeros((CONV_TAIL, CONV_WIDTH), F32)
        qkext[0:QK_TAIL, :] = jnp.zeros((QK_TAIL, 2 * MLSTM_WIDTH), F32)
        c_s[...] = jnp.zeros(c_s.shape, F32)
        n_s[...] = jnp.zeros(n_s.shape, F32)
        m_s[...] = jnp.full(m_s.shape, NEG, F32)

    t0 = t_idx * tt
    valid_col = (lax.broadcasted_iota(jnp.int32, (tt, 1), 0) + t0) >= n_pad
    valid_row = (lax.broadcasted_iota(jnp.int32, (1, tt), 1) + t0) >= n_pad

    x = x_ref[0]
    h = _rms_rows(x, g_ref[...]).astype(BF16)

    def proj(lo, hi):
        return _dot(h, win_ref[:, lo:hi])

    q = proj(OFF_AQ, OFF_AK)
    k = proj(OFF_AK, OFF_AV)
    v = proj(OFF_AV, OFF_CA)
    q = q * lax.rsqrt(_seg_sumsq(q, onesq_ref) * (1.0 / HEAD_DIM) + RMS_EPS) * qg_ref[...]
    q = q * (HEAD_DIM ** -0.5)
    k = k * lax.rsqrt(_seg_sumsq(k, onesk_ref) * (1.0 / HEAD_DIM) + RMS_EPS) * kg_ref[...]
    kext[BLOCK:BLOCK + tt, :] = k.astype(BF16)
    vext[BLOCK:BLOCK + tt, :] = v.astype(BF16)
    kbias[:, BLOCK:BLOCK + tt] = jnp.where(valid_row, 0.0, NEG).astype(F32)

    band = band_ref[...]
    for n in range(nblk):
        r0 = n * BLOCK
        bias = band + kbias[:, r0:r0 + 2 * BLOCK]
        kcat = kext[r0:r0 + 2 * BLOCK, :]
        vcat = vext[r0:r0 + 2 * BLOCK, :]
        qblk = q[r0:r0 + BLOCK, :]
        outs = []
        for hk in range(ATTN_KV_HEADS):
            qs = jnp.concatenate(
                [qblk[:, (hk * ATTN_GROUP + g) * HEAD_DIM:(hk * ATTN_GROUP + g + 1) * HEAD_DIM]
                 for g in range(ATTN_GROUP)], axis=0).astype(BF16)
            kh = kcat[:, hk * HEAD_DIM:(hk + 1) * HEAD_DIM]
            vh = vcat[:, hk * HEAD_DIM:(hk + 1) * HEAD_DIM]
            s = _dot_nt(qs, kh) + bias
            sink = jnp.concatenate(
                [jnp.full((BLOCK, 1), sinks_ref[hk * ATTN_GROUP + g], F32)
                 for g in range(ATTN_GROUP)], axis=0)
            m = jnp.maximum(jnp.max(s, axis=-1, keepdims=True), sink)
            p = jnp.exp(s - m)
            den = jnp.sum(p, axis=-1, keepdims=True) + jnp.exp(sink - m)
            o = _dot(p.astype(BF16), vh) / den
            outs.extend(o[g * BLOCK:(g + 1) * BLOCK, :] for g in range(ATTN_GROUP))
        ya_s[r0:r0 + BLOCK, :] = jnp.concatenate(outs, axis=1).astype(BF16)
    kext[0:BLOCK, :] = kext[tt:tt + BLOCK, :]
    vext[0:BLOCK, :] = vext[tt:tt + BLOCK, :]
    kbias[:, 0:BLOCK] = kbias[:, tt:tt + BLOCK]
    ya = _dot(ya_s[...], woa_ref[...])

    ca = proj(OFF_CA, OFF_CG)
    cg = proj(OFF_CG, OFF_MQ)
    u = jnp.where(valid_col, ca * jax.nn.sigmoid(cg), 0.0)
    uext[CONV_TAIL:CONV_TAIL + tt, :] = u
    base = CONV_TAIL - (CONV_KERNEL - 1)
    acc = jnp.zeros((tt, CONV_WIDTH), F32) + cb_ref[...]
    for j in range(CONV_KERNEL):
        acc = acc + uext[base + j:base + j + tt, :] * cw_ref[j:j + 1, :]
    uext[0:CONV_TAIL, :] = uext[tt:tt + CONV_TAIL, :]
    mu = jnp.mean(acc, axis=-1, keepdims=True)
    xc = acc - mu
    yn = xc * lax.rsqrt(jnp.mean(xc * xc, axis=-1, keepdims=True) + LN_EPS)
    yn = yn * lng_ref[...] + lnb_ref[...]
    yc = _dot((yn * jax.nn.sigmoid(yn)).astype(BF16), woc_ref[...])

    mqk = proj(OFF_MQ, OFF_MV)
    qkext[QK_TAIL:QK_TAIL + tt, :] = jnp.where(valid_col, mqk, 0.0)
    qbase = QK_TAIL - (QK_CONV_KERNEL - 1)
    qacc = jnp.zeros((tt, 2 * MLSTM_WIDTH), F32) + qkb_ref[...]
    for j in range(QK_CONV_KERNEL):
        qacc = qacc + qkext[qbase + j:qbase + j + tt, :] * qkw_ref[j:j + 1, :]
    qkext[0:QK_TAIL, :] = qkext[tt:tt + QK_TAIL, :]
    qk = qacc * jax.nn.sigmoid(qacc)
    qm_s[...] = qk[:, :MLSTM_WIDTH] * (MLSTM_HEAD_DIM ** -0.5)
    km_s[...] = qk[:, MLSTM_WIDTH:]
    vm_s[...] = proj(OFF_MV, OFF_MO)

    zif = proj(OFF_IF, N_IN_PACKED) + ifb_ref[...]
    log_i = jnp.where(valid_col, zif, NEG)
    log_f = pltpu.roll(jax.nn.log_sigmoid(zif), V7X_LANES - MLSTM_HEADS, axis=1)
    pos = lax.broadcasted_iota(jnp.int32, (tt, 1), 0) % MLSTM_CHUNK
    bcum = log_f
    d = 1
    while d < MLSTM_CHUNK:
        bcum = bcum + jnp.where(pos >= d, pltpu.roll(bcum, d, axis=0), 0.0)
        d *= 2
    bcum_s[...] = bcum
    r_s[...] = log_i - bcum

    tri = (lax.broadcasted_iota(jnp.int32, (MLSTM_CHUNK, MLSTM_CHUNK), 0) >=
           lax.broadcasted_iota(jnp.int32, (MLSTM_CHUNK, MLSTM_CHUNK), 1))

    def chunk_body(c, carry):
        r0 = pl.multiple_of(c * MLSTM_CHUNK, MLSTM_CHUNK)
        rows = pl.ds(r0, MLSTM_CHUNK)
        bc = bcum_s[rows, :]
        rc = r_s[rows, :]
        rct = rc.T
        for hd in range(MLSTM_HEADS):
            lanes = slice(hd * MLSTM_HEAD_DIM, (hd + 1) * MLSTM_HEAD_DIM)
            qc = qm_s[rows, lanes]
            kc = km_s[rows, lanes]
            vc = vm_s[rows, lanes]
            bcol = bc[:, hd:hd + 1]
            rcol = rc[:, hd:hd + 1]
            rrow = rct[hd:hd + 1, :]
            b_end = bc[MLSTM_CHUNK - 1:MLSTM_CHUNK, hd:hd + 1]
            m_prev = m_s[hd:hd + 1, 0:1]
            n_prev = n_s[hd:hd + 1, :]
            c_prev = c_s[hd]

            log_d = jnp.where(tri, bcol + rrow, NEG)
            m_inter = bcol + m_prev
            m_t = jnp.maximum(m_inter, jnp.max(log_d, axis=-1, keepdims=True))
            w_inter = jnp.exp(m_inter - m_t)
            qb = qc.astype(BF16)
            kb = kc.astype(BF16)
            vb = vc.astype(BF16)
            s = _dot_nt(qb, kb) * jnp.exp(log_d - m_t)
            num = _dot(s.astype(BF16), vb) + w_inter * _dot(qb, c_prev.astype(BF16))
            den = (jnp.sum(s, axis=-1, keepdims=True) +
                   w_inter * jnp.sum(qc * n_prev, axis=-1, keepdims=True))
            ht_s[rows, lanes] = num / jnp.maximum(jnp.abs(den), jnp.exp(-m_t))

            log_e = b_end + rcol
            m_loc = jnp.max(log_e, axis=0, keepdims=True)
            w_e = jnp.exp(log_e - m_loc)
            m_new = jnp.maximum(b_end + m_prev, m_loc)
            a = jnp.exp(b_end + m_prev - m_new)
            cc = jnp.exp(m_loc - m_new)
            kw = kc * w_e
            c_s[hd] = a * c_prev + cc * _dot_tn(kw.astype(BF16), vb)
            n_s[hd:hd + 1, :] = a * n_prev + cc * jnp.sum(kw, axis=0, keepdims=True)
            m_s[hd:hd + 1, :] = jnp.broadcast_to(m_new, (1, V7X_LANES))
        return carry

    lax.fori_loop(0, nchunk, chunk_body, 0)

    mo = proj(OFF_MO, OFF_ZG)
    ym = _dot((jax.nn.sigmoid(mo) * ht_s[...]).astype(BF16), wom_ref[...])

    def gate(i, y):
        zg = proj(OFF_ZG + i * D_MODEL, OFF_ZG + (i + 1) * D_MODEL) + gb_ref[i:i + 1, :]
        return jax.nn.sigmoid(zg) * y

    y = gate(0, ya) + gate(1, yc) + gate(2, ym)
    o_ref[0] = x + _dot(y.astype(BF16), wout_ref[...])


def _mixer(hseq, p, consts, *, n_pad):
    bsz, t_len, _ = hseq.shape
    tt = MIXER_TIME_TILE
    assert t_len % tt == 0 and tt % BLOCK == 0 and tt % MLSTM_CHUNK == 0
    assert tt >= CONV_TAIL and tt >= BLOCK
    body = functools.partial(_mixer_body, tt=tt, n_pad=n_pad)
    in_specs = [
        pl.BlockSpec(memory_space=pltpu.SMEM),
        pl.BlockSpec((1, tt, D_MODEL), lambda b, t: (b, t, 0)),
        _const_spec((1, D_MODEL)),
        _const_spec((D_MODEL, N_IN_PACKED)),
        _const_spec((N_BRANCH, D_MODEL)),
        _const_spec((1, ATTN_WIDTH)),
        _const_spec((1, KV_WIDTH)),
        _const_spec((ATTN_WIDTH, D_MODEL)),
        _const_spec((CONV_KERNEL, CONV_WIDTH)),
        _const_spec((1, CONV_WIDTH)),
        _const_spec((1, CONV_WIDTH)),
        _const_spec((1, CONV_WIDTH)),
        _const_spec((CONV_WIDTH, D_MODEL)),
        _const_spec((QK_CONV_KERNEL, 2 * MLSTM_WIDTH)),
        _const_spec((1, 2 * MLSTM_WIDTH)),
        _const_spec((1, V7X_LANES)),
        _const_spec((MLSTM_WIDTH, D_MODEL)),
        _const_spec((D_MODEL, D_MODEL)),
        _const_spec((ATTN_WIDTH, ATTN_WIDTH)),
        _const_spec((KV_WIDTH, KV_WIDTH)),
        _const_spec((ATTN_GROUP * BLOCK, 2 * BLOCK)),
    ]
    scratch = [
        pltpu.VMEM((BLOCK + tt, KV_WIDTH), BF16),
        pltpu.VMEM((BLOCK + tt, KV_WIDTH), BF16),
        pltpu.VMEM((1, BLOCK + tt), F32),
        pltpu.VMEM((CONV_TAIL + tt, CONV_WIDTH), F32),
        pltpu.VMEM((QK_TAIL + tt, 2 * MLSTM_WIDTH), F32),
        pltpu.VMEM((tt, MLSTM_WIDTH), F32),
        pltpu.VMEM((tt, MLSTM_WIDTH), F32),
        pltpu.VMEM((tt, MLSTM_WIDTH), F32),
        pltpu.VMEM((tt, MLSTM_WIDTH), F32),
        pltpu.VMEM((tt, V7X_LANES), F32),
        pltpu.VMEM((tt, V7X_LANES), F32),
        pltpu.VMEM((tt, ATTN_WIDTH), BF16),
        pltpu.VMEM((MLSTM_HEADS, MLSTM_HEAD_DIM, MLSTM_HEAD_DIM), F32),
        pltpu.VMEM((V7X_SUBLANES, MLSTM_HEAD_DIM), F32),
        pltpu.VMEM((V7X_SUBLANES, V7X_LANES), F32),
    ]
    return pl.pallas_call(
        body,
        grid=(bsz, t_len // tt),
        in_specs=in_specs,
        out_specs=pl.BlockSpec((1, tt, D_MODEL), lambda b, t: (b, t, 0)),
        out_shape=jax.ShapeDtypeStruct(hseq.shape, F32),
        scratch_shapes=scratch,
        compiler_params=pltpu.CompilerParams(
            dimension_semantics=("arbitrary", "arbitrary"),
            vmem_limit_bytes=MIXER_VMEM_LIMIT),
        name="mixer",
    )(p["sinks"], hseq, p["norm"], p["w_in"], p["gate_bias"], p["q_gain"], p["k_gain"],
      p["w_o_attn"], p["conv_w"], p["conv_b"], p["ln_g"], p["ln_b"], p["w_o_conv"],
      p["qk_w"], p["qk_b"], p["if_bias"], p["w_o_mlstm"], p["w_out"],
      consts["ones_q"], consts["ones_k"], consts["band"])


def _make_consts():
    def blockdiag(n):
        i = jnp.arange(n) // HEAD_DIM
        return (i[:, None] == i[None, :]).astype(BF16)

    qi = jnp.arange(ATTN_GROUP * BLOCK) % BLOCK
    kj = jnp.arange(2 * BLOCK)
    allowed = (kj[None, :] > qi[:, None]) & (kj[None, :] <= qi[:, None] + BLOCK)
    return {
        "ones_q": blockdiag(ATTN_WIDTH),
        "ones_k": blockdiag(KV_WIDTH),
        "band": jnp.where(allowed, 0.0, NEG).astype(F32),
    }


def _pack_w_in(w_in):
    pad = jnp.zeros((D_MODEL, V7X_LANES - 2 * MLSTM_HEADS), w_in.dtype)
    return jnp.concatenate(
        [w_in[:, :ORIG_OFF_IF], w_in[:, ORIG_OFF_ZG:], w_in[:, ORIG_OFF_IF:ORIG_OFF_ZG], pad],
        axis=1).astype(BF16)


def kernel(x, meta_tokens, ffn1_norm, ffn1_w1, ffn1_w3, ffn1_w2, mix_norm, w_in, gate_bias,
           attn_q_norm, attn_k_norm, attn_sinks, w_o_attn, conv_dw_w, conv_dw_b, conv_ln_g,
           conv_ln_b, w_o_conv, mlstm_qk_conv_w, mlstm_qk_conv_b, mlstm_igate_bias,
           mlstm_fgate_bias, w_o_mlstm, w_out, ffn2_norm, ffn2_w1, ffn2_w3, ffn2_w2):
    bsz, seq, _ = x.shape
    depth = w_in.shape[0]
    lead = jnp.concatenate([
        jnp.zeros((bsz, N_PAD, D_MODEL), x.dtype),
        jnp.broadcast_to(meta_tokens.astype(x.dtype)[None], (bsz, N_META, D_MODEL))], axis=1)
    h = jnp.concatenate([lead, x], axis=1)
    t_len = h.shape[1]
    consts = _make_consts()

    def row(v):
        return v.reshape(1, -1).astype(F32)

    for l in range(depth):
        if_bias = jnp.concatenate([
            mlstm_igate_bias[l], mlstm_fgate_bias[l],
            jnp.zeros((V7X_LANES - 2 * MLSTM_HEADS,), F32)]).reshape(1, V7X_LANES)
        mix = {
            "sinks": attn_sinks[l].astype(F32),
            "norm": row(mix_norm[l]),
            "w_in": _pack_w_in(w_in[l]),
            "gate_bias": gate_bias[l].astype(F32),
            "q_gain": row(jnp.tile(attn_q_norm[l], ATTN_HEADS)),
            "k_gain": row(jnp.tile(attn_k_norm[l], ATTN_KV_HEADS)),
            "w_o_attn": w_o_attn[l].astype(BF16),
            "conv_w": conv_dw_w[l].astype(F32),
            "conv_b": row(conv_dw_b[l]),
            "ln_g": row(conv_ln_g[l]),
            "ln_b": row(conv_ln_b[l]),
            "w_o_conv": w_o_conv[l].astype(BF16),
            "qk_w": mlstm_qk_conv_w[l].astype(F32),
            "qk_b": row(mlstm_qk_conv_b[l]),
            "if_bias": if_bias,
            "w_o_mlstm": w_o_mlstm[l].astype(BF16),
            "w_out": w_out[l].astype(BF16),
        }
        h2 = h.reshape(bsz * t_len, D_MODEL)
        h2 = _ffn(h2, row(ffn1_norm[l]), ffn1_w1[l].astype(BF16), ffn1_w3[l].astype(BF16),
                  ffn1_w2[l].astype(BF16))
        h = _mixer(h2.reshape(bsz, t_len, D_MODEL), mix, consts, n_pad=N_PAD)
        h2 = h.reshape(bsz * t_len, D_MODEL)
        h2 = _ffn(h2, row(ffn2_norm[l]), ffn2_w1[l].astype(BF16), ffn2_w3[l].astype(BF16),
                  ffn2_w2[l].astype(BF16))
        h = h2.reshape(bsz, t_len, D_MODEL)
    return h[:, LEAD:]
```

```python
import functools
import math

import jax
import jax.numpy as jnp
from jax import lax
from jax.experimental import pallas as pl
from jax.experimental.pallas import tpu as pltpu

D_MODEL = 1024
N_META = 16
BLOCK = 128
LEAD = BLOCK
N_PAD = LEAD - N_META
ATTN_HEADS = 8
ATTN_KV_HEADS = 2
HEAD_DIM = 64
ATTN_GROUP = ATTN_HEADS // ATTN_KV_HEADS
ATTN_WIDTH = ATTN_HEADS * HEAD_DIM
KV_WIDTH = ATTN_KV_HEADS * HEAD_DIM
CONV_WIDTH = 512
CONV_KERNEL = 31
MLSTM_HEADS = 4
MLSTM_WIDTH = 512
MLSTM_HEAD_DIM = MLSTM_WIDTH // MLSTM_HEADS
MLSTM_CHUNK = 64
QK_CONV_KERNEL = 4
D_FF = 2816
N_BRANCH = 3
RMS_EPS = 1e-6
LN_EPS = 1e-5
NEG = -1e30

V7X_LANES = 128
V7X_SUBLANES = 8
V7X_VMEM_BYTES = 64 * 1024 * 1024

ORIG_OFF_AQ = 0
ORIG_OFF_AK = ORIG_OFF_AQ + ATTN_WIDTH
ORIG_OFF_AV = ORIG_OFF_AK + KV_WIDTH
ORIG_OFF_CA = ORIG_OFF_AV + KV_WIDTH
ORIG_OFF_IF = ORIG_OFF_CA + 2 * CONV_WIDTH + 4 * MLSTM_WIDTH
ORIG_OFF_ZG = ORIG_OFF_IF + 2 * MLSTM_HEADS

OFF_AK = 0
OFF_CA = OFF_AK + KV_WIDTH
OFF_CG = OFF_CA + CONV_WIDTH
OFF_MQ = OFF_CG + CONV_WIDTH
OFF_MK = OFF_MQ + MLSTM_WIDTH
OFF_MV = OFF_MK + MLSTM_WIDTH
OFF_MO = OFF_MV + MLSTM_WIDTH
OFF_ZG = OFF_MO + MLSTM_WIDTH
OFF_IF = OFF_ZG + N_BRANCH * D_MODEL
N_IN_PACKED = OFF_IF + V7X_LANES

CONV_TAIL = 32
CONV_SHIFT_PAD = CONV_TAIL - V7X_SUBLANES
QK_TAIL = 8

FFN_ROW_TILE = 512
MIXER_TIME_TILE = 384
FFN_VMEM_LIMIT = 56 * 1024 * 1024
MIXER_VMEM_LIMIT = 60 * 1024 * 1024

F32 = jnp.float32
BF16 = jnp.bfloat16


def _const_spec(shape):
    nd = len(shape)
    return pl.BlockSpec(shape, lambda *_: (0,) * nd, pipeline_mode=pl.Buffered(1))


def _dot(a, b):
    return jnp.dot(a, b, preferred_element_type=F32)


def _dot_nt(a, b):
    return lax.dot_general(a, b, (((1,), (1,)), ((), ())), preferred_element_type=F32)


def _dot_tn(a, b):
    return lax.dot_general(a, b, (((0,), (0,)), ((), ())), preferred_element_type=F32)


def _rms_rows(x, gain):
    ms = jnp.mean(x * x, axis=-1, keepdims=True)
    return x * lax.rsqrt(ms + RMS_EPS) * gain


def _seg_sumsq(x, ones_ref):
    x2 = x * x
    hi = x2.astype(BF16)
    lo = (x2 - hi.astype(F32)).astype(BF16)
    ones = ones_ref[...]
    return _dot(hi, ones) + _dot(lo, ones)


def _ffn_body(x_ref, g_ref, w1_ref, w3_ref, w2_ref, o_ref):
    x = x_ref[...]
    h = _rms_rows(x, g_ref[...]).astype(BF16)
    a = _dot(h, w1_ref[...])
    b = _dot(h, w3_ref[...])
    u = (a * jax.nn.sigmoid(a) * b).astype(BF16)
    o_ref[...] = x + 0.5 * _dot(u, w2_ref[...])


def _ffn(x2d, g, w1, w3, w2):
    rows = x2d.shape[0]
    tm = FFN_ROW_TILE
    assert rows % tm == 0
    return pl.pallas_call(
        _ffn_body,
        grid=(rows // tm,),
        in_specs=[
            pl.BlockSpec((tm, D_MODEL), lambda i: (i, 0)),
            _const_spec((1, D_MODEL)),
            _const_spec((D_MODEL, D_FF)),
            _const_spec((D_MODEL, D_FF)),
            _const_spec((D_FF, D_MODEL)),
        ],
        out_specs=pl.BlockSpec((tm, D_MODEL), lambda i: (i, 0)),
        out_shape=jax.ShapeDtypeStruct(x2d.shape, F32),
        compiler_params=pltpu.CompilerParams(
            dimension_semantics=("arbitrary",), vmem_limit_bytes=FFN_VMEM_LIMIT),
        name="ffn",
    )(x2d, g, w1, w3, w2)


def _mixer_body(x_ref, g_ref, win_ref, wqt_ref, wvt_ref, gb_ref, qg_ref, kg_ref, sink_ref,
                woa_ref, cw_ref, cb_ref, lng_ref, lnb_ref, woc_ref, qkw_ref, qkb_ref, ifb_ref,
                wom_ref, wout_ref, onesk_ref, band_ref,
                o_ref,
                kext, vext_t, uext, ush, qkext, qm_s, km_s, vm_s, ht_s, bcum_s, r_s, ya_s,
                c_s, n_s, m_s, *, tt, n_pad):
    t_idx = pl.program_id(1)
    nblk = tt // BLOCK
    nchunk = tt // MLSTM_CHUNK

    @pl.when(t_idx == 0)
    def _init_state():
        kext[0:BLOCK, :] = jnp.zeros((BLOCK, KV_WIDTH), BF16)
        vext_t[:, 0:BLOCK] = jnp.zeros((KV_WIDTH, BLOCK), BF16)
        uext[0:CONV_TAIL, :] = jnp.zeros((CONV_TAIL, CONV_WIDTH), F32)
        qkext[0:QK_TAIL, :] = jnp.zeros((QK_TAIL, 2 * MLSTM_WIDTH), F32)
        c_s[...] = jnp.zeros(c_s.shape, F32)
        n_s[...] = jnp.zeros(n_s.shape, F32)
        m_s[...] = jnp.full(m_s.shape, NEG, F32)

    t0 = t_idx * tt
    valid_col = (lax.broadcasted_iota(jnp.int32, (tt, 1), 0) + t0) >= n_pad

    x = x_ref[0]
    h = _rms_rows(x, g_ref[...]).astype(BF16)

    def proj(lo, hi):
        return _dot(h, win_ref[:, lo:hi])

    qt = _dot_nt(wqt_ref[...], h)
    vt = _dot_nt(wvt_ref[...], h)
    k = proj(OFF_AK, OFF_CA)
    k = k * lax.rsqrt(_seg_sumsq(k, onesk_ref) * (1.0 / HEAD_DIM) + RMS_EPS) * kg_ref[...]
    kext[BLOCK:BLOCK + tt, :] = k.astype(BF16)
    vext_t[:, BLOCK:BLOCK + tt] = vt.astype(BF16)

    qn = []
    for a in range(ATTN_HEADS):
        qa = qt[a * HEAD_DIM:(a + 1) * HEAD_DIM, :]
        inv = lax.rsqrt(jnp.mean(qa * qa, axis=0, keepdims=True) + RMS_EPS) * (HEAD_DIM ** -0.5)
        qn.append(qa * inv)
    qgain = qg_ref[...]
    band = band_ref[...]
    zeros_q = jnp.zeros((HEAD_DIM, ATTN_GROUP * BLOCK), BF16)
    key_iota = lax.broadcasted_iota(jnp.int32, (2 * BLOCK, ATTN_GROUP * BLOCK), 0)
    for n in range(nblk):
        r0 = n * BLOCK
        key_pos = key_iota + (t0 + r0 - BLOCK)
        bias = jnp.where(key_pos >= n_pad, band, NEG)
        kcat = kext[r0:r0 + 2 * BLOCK, :]
        for hk in range(ATTN_KV_HEADS):
            qh = jnp.concatenate(
                [(qn[hk * ATTN_GROUP + g][:, r0:r0 + BLOCK] * qgain).astype(BF16)
                 for g in range(ATTN_GROUP)], axis=1)
            rhs = jnp.concatenate([qh, zeros_q] if hk == 0 else [zeros_q, qh], axis=0)
            st = _dot(kcat, rhs) + bias
            sink = sink_ref[hk:hk + 1, :]
            m = jnp.maximum(jnp.max(st, axis=0, keepdims=True), sink)
            p = jnp.exp(st - m)
            den = jnp.sum(p, axis=0, keepdims=True) + jnp.exp(sink - m)
            vth = vext_t[hk * HEAD_DIM:(hk + 1) * HEAD_DIM, r0:r0 + 2 * BLOCK]
            ot = _dot(vth, p.astype(BF16)) / den
            for j in range(ATTN_GROUP // 2):
                pair = jnp.concatenate(
                    [ot[:, (2 * j) * BLOCK:(2 * j + 1) * BLOCK],
                     ot[:, (2 * j + 1) * BLOCK:(2 * j + 2) * BLOCK]], axis=0)
                c0 = (hk * ATTN_GROUP + 2 * j) * HEAD_DIM
                ya_s[r0:r0 + BLOCK, c0:c0 + 2 * HEAD_DIM] = pair.T.astype(BF16)
    kext[0:BLOCK, :] = kext[tt:tt + BLOCK, :]
    vext_t[:, 0:BLOCK] = vext_t[:, tt:tt + BLOCK]
    ya = _dot(ya_s[...], woa_ref[...])

    ca = proj(OFF_CA, OFF_CG)
    cg = proj(OFF_CG, OFF_MQ)
    u = jnp.where(valid_col, ca * jax.nn.sigmoid(cg), 0.0)
    uext[CONV_TAIL:CONV_TAIL + tt, :] = u
    for s in range(1, V7X_SUBLANES):
        ush[s - 1] = uext[s:s + tt + CONV_SHIFT_PAD, :]
    base = CONV_TAIL - (CONV_KERNEL - 1)
    acc = jnp.zeros((tt, CONV_WIDTH), F32) + cb_ref[...]
    for j in range(CONV_KERNEL):
        off = base + j
        s, a = off % V7X_SUBLANES, (off // V7X_SUBLANES) * V7X_SUBLANES
        tap = uext[a:a + tt, :] if s == 0 else ush[s - 1, a:a + tt, :]
        acc = acc + tap * cw_ref[j:j + 1, :]
    uext[0:CONV_TAIL, :] = uext[tt:tt + CONV_TAIL, :]
    mu = jnp.mean(acc, axis=-1, keepdims=True)
    xc = acc - mu
    yn = xc * lax.rsqrt(jnp.mean(xc * xc, axis=-1, keepdims=True) + LN_EPS)
    yn = yn * lng_ref[...] + lnb_ref[...]
    yc = _dot((yn * jax.nn.sigmoid(yn)).astype(BF16), woc_ref[...])

    mqk = proj(OFF_MQ, OFF_MV)
    qkext[QK_TAIL:QK_TAIL + tt, :] = jnp.where(valid_col, mqk, 0.0)
    qbase = QK_TAIL - (QK_CONV_KERNEL - 1)
    qacc = jnp.zeros((tt, 2 * MLSTM_WIDTH), F32) + qkb_ref[...]
    for j in range(QK_CONV_KERNEL):
        qacc = qacc + qkext[qbase + j:qbase + j + tt, :] * qkw_ref[j:j + 1, :]
    qkext[0:QK_TAIL, :] = qkext[tt:tt + QK_TAIL, :]
    qk = qacc * jax.nn.sigmoid(qacc)
    qm_s[...] = qk[:, :MLSTM_WIDTH] * (MLSTM_HEAD_DIM ** -0.5)
    km_s[...] = qk[:, MLSTM_WIDTH:]
    vm_s[...] = proj(OFF_MV, OFF_MO)

    zif = proj(OFF_IF, N_IN_PACKED) + ifb_ref[...]
    log_i = jnp.where(valid_col, zif, NEG)
    log_f = pltpu.roll(jax.nn.log_sigmoid(zif), V7X_LANES - MLSTM_HEADS, axis=1)
    pos = lax.broadcasted_iota(jnp.int32, (tt, 1), 0) % MLSTM_CHUNK
    bcum = log_f
    d = 1
    while d < MLSTM_CHUNK:
        bcum = bcum + jnp.where(pos >= d, pltpu.roll(bcum, d, axis=0), 0.0)
        d *= 2
    bcum_s[...] = bcum
    r_s[...] = log_i - bcum

    tri = (lax.broadcasted_iota(jnp.int32, (MLSTM_CHUNK, MLSTM_CHUNK), 0) >=
           lax.broadcasted_iota(jnp.int32, (MLSTM_CHUNK, MLSTM_CHUNK), 1))

    def chunk_body(c, carry):
        r0 = pl.multiple_of(c * MLSTM_CHUNK, MLSTM_CHUNK)
        rows = pl.ds(r0, MLSTM_CHUNK)
        bc = bcum_s[rows, :]
        rc = r_s[rows, :]
        rct = rc.T
        for hd in range(MLSTM_HEADS):
            lanes = slice(hd * MLSTM_HEAD_DIM, (hd + 1) * MLSTM_HEAD_DIM)
            qc = qm_s[rows, lanes]
            kc = km_s[rows, lanes]
            vc = vm_s[rows, lanes]
            bcol = bc[:, hd:hd + 1]
            rcol = rc[:, hd:hd + 1]
            rrow = rct[hd:hd + 1, :]
            b_end = bc[MLSTM_CHUNK - 1:MLSTM_CHUNK, hd:hd + 1]
            m_prev = m_s[hd:hd + 1, 0:1]
            n_prev = n_s[hd:hd + 1, :]
            c_prev = c_s[hd]

            log_d = jnp.where(tri, bcol + rrow, NEG)
            m_inter = bcol + m_prev
            m_t = jnp.maximum(m_inter, jnp.max(log_d, axis=-1, keepdims=True))
            w_inter = jnp.exp(m_inter - m_t)
            qb = qc.astype(BF16)
            kb = kc.astype(BF16)
            vb = vc.astype(BF16)
            s = _dot_nt(qb, kb) * jnp.exp(log_d - m_t)
            num = _dot(s.astype(BF16), vb) + w_inter * _dot(qb, c_prev.astype(BF16))
            den = (jnp.sum(s, axis=-1, keepdims=True) +
                   w_inter * jnp.sum(qc * n_prev, axis=-1, keepdims=True))
            ht_s[rows, lanes] = num / jnp.maximum(jnp.abs(den), jnp.exp(-m_t))

            log_e = b_end + rcol
            m_loc = jnp.max(log_e, axis=0, keepdims=True)
            w_e = jnp.exp(log_e - m_loc)
            m_new = jnp.maximum(b_end + m_prev, m_loc)
            a = jnp.exp(b_end + m_prev - m_new)
            cc = jnp.exp(m_loc - m_new)
            kw = kc * w_e
            c_s[hd] = a * c_prev + cc * _dot_tn(kw.astype(BF16), vb)
            n_s[hd:hd + 1, :] = a * n_prev + cc * jnp.sum(kw, axis=0, keepdims=True)
            m_s[hd:hd + 1, :] = jnp.broadcast_to(m_new, (1, V7X_LANES))
        return carry

    lax.fori_loop(0, nchunk, chunk_body, 0)

    mo = proj(OFF_MO, OFF_ZG)
    ym = _dot((jax.nn.sigmoid(mo) * ht_s[...]).astype(BF16), wom_ref[...])

    def gate(i, y):
        zg = proj(OFF_ZG + i * D_MODEL, OFF_ZG + (i + 1) * D_MODEL) + gb_ref[i:i + 1, :]
        return jax.nn.sigmoid(zg) * y

    y = gate(0, ya) + gate(1, yc) + gate(2, ym)
    o_ref[0] = x + _dot(y.astype(BF16), wout_ref[...])


def _mixer(hseq, p, consts, *, n_pad):
    bsz, t_len, _ = hseq.shape
    tt = MIXER_TIME_TILE
    assert t_len % tt == 0 and tt % BLOCK == 0 and tt % MLSTM_CHUNK == 0
    assert tt >= CONV_TAIL and tt >= BLOCK
    body = functools.partial(_mixer_body, tt=tt, n_pad=n_pad)
    in_specs = [
        pl.BlockSpec((1, tt, D_MODEL), lambda b, t: (b, t, 0)),
        _const_spec((1, D_MODEL)),
        _const_spec((D_MODEL, N_IN_PACKED)),
        _const_spec((ATTN_WIDTH, D_MODEL)),
        _const_spec((KV_WIDTH, D_MODEL)),
        _const_spec((N_BRANCH, D_MODEL)),
        _const_spec((HEAD_DIM, BLOCK)),
        _const_spec((1, KV_WIDTH)),
        _const_spec((ATTN_KV_HEADS, ATTN_GROUP * BLOCK)),
        _const_spec((ATTN_WIDTH, D_MODEL)),
        _const_spec((CONV_KERNEL, CONV_WIDTH)),
        _const_spec((1, CONV_WIDTH)),
        _const_spec((1, CONV_WIDTH)),
        _const_spec((1, CONV_WIDTH)),
        _const_spec((CONV_WIDTH, D_MODEL)),
        _const_spec((QK_CONV_KERNEL, 2 * MLSTM_WIDTH)),
        _const_spec((1, 2 * MLSTM_WIDTH)),
        _const_spec((1, V7X_LANES)),
        _const_spec((MLSTM_WIDTH, D_MODEL)),
        _const_spec((D_MODEL, D_MODEL)),
        _const_spec((KV_WIDTH, KV_WIDTH)),
        _const_spec((2 * BLOCK, ATTN_GROUP * BLOCK)),
    ]
    scratch = [
        pltpu.VMEM((BLOCK + tt, KV_WIDTH), BF16),
        pltpu.VMEM((KV_WIDTH, BLOCK + tt), BF16),
        pltpu.VMEM((CONV_TAIL + tt, CONV_WIDTH), F32),
        pltpu.VMEM((V7X_SUBLANES - 1, tt + CONV_SHIFT_PAD, CONV_WIDTH), F32),
        pltpu.VMEM((QK_TAIL + tt, 2 * MLSTM_WIDTH), F32),
        pltpu.VMEM((tt, MLSTM_WIDTH), F32),
        pltpu.VMEM((tt, MLSTM_WIDTH), F32),
        pltpu.VMEM((tt, MLSTM_WIDTH), F32),
        pltpu.VMEM((tt, MLSTM_WIDTH), F32),
        pltpu.VMEM((tt, V7X_LANES), F32),
        pltpu.VMEM((tt, V7X_LANES), F32),
        pltpu.VMEM((tt, ATTN_WIDTH), BF16),
        pltpu.VMEM((MLSTM_HEADS, MLSTM_HEAD_DIM, MLSTM_HEAD_DIM), F32),
        pltpu.VMEM((V7X_SUBLANES, MLSTM_HEAD_DIM), F32),
        pltpu.VMEM((V7X_SUBLANES, V7X_LANES), F32),
    ]
    return pl.pallas_call(
        body,
        grid=(bsz, t_len // tt),
        in_specs=in_specs,
        out_specs=pl.BlockSpec((1, tt, D_MODEL), lambda b, t: (b, t, 0)),
        out_shape=jax.ShapeDtypeStruct(hseq.shape, F32),
        scratch_shapes=scratch,
        compiler_params=pltpu.CompilerParams(
            dimension_semantics=("arbitrary", "arbitrary"),
            vmem_limit_bytes=MIXER_VMEM_LIMIT),
        name="mixer",
    )(hseq, p["norm"], p["w_in"], p["w_q_t"], p["w_v_t"], p["gate_bias"], p["q_gain"],
      p["k_gain"], p["sinks"], p["w_o_attn"], p["conv_w"], p["conv_b"], p["ln_g"], p["ln_b"],
      p["w_o_conv"], p["qk_w"], p["qk_b"], p["if_bias"], p["w_o_mlstm"], p["w_out"],
      consts["ones_k"], consts["band"])


def _make_consts():
    i = jnp.arange(KV_WIDTH) // HEAD_DIM
    kj = jnp.arange(2 * BLOCK)
    qi = jnp.arange(ATTN_GROUP * BLOCK) % BLOCK
    allowed = (kj[:, None] > qi[None, :]) & (kj[:, None] <= qi[None, :] + BLOCK)
    return {
        "ones_k": (i[:, None] == i[None, :]).astype(BF16),
        "band": jnp.where(allowed, 0.0, NEG).astype(F32),
    }


def _pack_w_in(w_in):
    pad = jnp.zeros((D_MODEL, V7X_LANES - 2 * MLSTM_HEADS), w_in.dtype)
    return jnp.concatenate(
        [w_in[:, ORIG_OFF_AK:ORIG_OFF_AV], w_in[:, ORIG_OFF_CA:ORIG_OFF_IF], w_in[:, ORIG_OFF_ZG:],
         w_in[:, ORIG_OFF_IF:ORIG_OFF_ZG], pad], axis=1).astype(BF16)


def kernel(x, meta_tokens, ffn1_norm, ffn1_w1, ffn1_w3, ffn1_w2, mix_norm, w_in, gate_bias,
           attn_q_norm, attn_k_norm, attn_sinks, w_o_attn, conv_dw_w, conv_dw_b, conv_ln_g,
           conv_ln_b, w_o_conv, mlstm_qk_conv_w, mlstm_qk_conv_b, mlstm_igate_bias,
           mlstm_fgate_bias, w_o_mlstm, w_out, ffn2_norm, ffn2_w1, ffn2_w3, ffn2_w2):
    bsz, seq, _ = x.shape
    depth = w_in.shape[0]
    lead = jnp.concatenate([
        jnp.zeros((bsz, N_PAD, D_MODEL), x.dtype),
        jnp.broadcast_to(meta_tokens.astype(x.dtype)[None], (bsz, N_META, D_MODEL))], axis=1)
    h = jnp.concatenate([lead, x], axis=1)
    t_len = h.shape[1]
    consts = _make_consts()

    def row(v):
        return v.reshape(1, -1).astype(F32)

    for l in range(depth):
        if_bias = jnp.concatenate([
            mlstm_igate_bias[l], mlstm_fgate_bias[l],
            jnp.zeros((V7X_LANES - 2 * MLSTM_HEADS,), F32)]).reshape(1, V7X_LANES)
        mix = {
            "norm": row(mix_norm[l]),
            "w_in": _pack_w_in(w_in[l]),
            "w_q_t": w_in[l][:, ORIG_OFF_AQ:ORIG_OFF_AK].T.astype(BF16),
            "w_v_t": w_in[l][:, ORIG_OFF_AV:ORIG_OFF_CA].T.astype(BF16),
            "gate_bias": gate_bias[l].astype(F32),
            "q_gain": jnp.broadcast_to(attn_q_norm[l].astype(F32)[:, None], (HEAD_DIM, BLOCK)),
            "k_gain": row(jnp.tile(attn_k_norm[l], ATTN_KV_HEADS)),
            "sinks": jnp.repeat(attn_sinks[l].astype(F32), BLOCK).reshape(
                ATTN_KV_HEADS, ATTN_GROUP * BLOCK),
            "w_o_attn": w_o_attn[l].astype(BF16),
            "conv_w": conv_dw_w[l].astype(F32),
            "conv_b": row(conv_dw_b[l]),
            "ln_g": row(conv_ln_g[l]),
            "ln_b": row(conv_ln_b[l]),
            "w_o_conv": w_o_conv[l].astype(BF16),
            "qk_w": mlstm_qk_conv_w[l].astype(F32),
            "qk_b": row(mlstm_qk_conv_b[l]),
            "if_bias": if_bias,
            "w_o_mlstm": w_o_mlstm[l].astype(BF16),
            "w_out": w_out[l].astype(BF16),
        }
        h2 = h.reshape(bsz * t_len, D_MODEL)
        h2 = _ffn(h2, row(ffn1_norm[l]), ffn1_w1[l].astype(BF16), ffn1_w3[l].astype(BF16),
                  ffn1_w2[l].astype(BF16))
        h = _mixer(h2.reshape(bsz, t_len, D_MODEL), mix, consts, n_pad=N_PAD)
        h2 = h.reshape(bsz * t_len, D_MODEL)
        h2 = _ffn(h2, row(ffn2_norm[l]), ffn2_w1[l].astype(BF16), ffn2_w3[l].astype(BF16),
                  ffn2_w2[l].astype(BF16))
        h = h2.reshape(bsz, t_len, D_MODEL)
    return h[:, LEAD:]
```

```python
import functools
import math

import jax
import jax.numpy as jnp
from jax import lax
from jax.experimental import pallas as pl
from jax.experimental.pallas import tpu as pltpu

D_MODEL = 1024
N_META = 16
BLOCK = 128
LEAD = BLOCK
N_PAD = LEAD - N_META
ATTN_HEADS = 8
ATTN_KV_HEADS = 2
HEAD_DIM = 64
ATTN_GROUP = ATTN_HEADS // ATTN_KV_HEADS
ATTN_WIDTH = ATTN_HEADS * HEAD_DIM
KV_WIDTH = ATTN_KV_HEADS * HEAD_DIM
CONV_WIDTH = 512
CONV_KERNEL = 31
MLSTM_HEADS = 4
MLSTM_WIDTH = 512
MLSTM_HEAD_DIM = MLSTM_WIDTH // MLSTM_HEADS
MLSTM_CHUNK = 64
QK_CONV_KERNEL = 4
D_FF = 2816
N_BRANCH = 3
RMS_EPS = 1e-6
LN_EPS = 1e-5
NEG = -1e30

V7X_LANES = 128
V7X_SUBLANES = 8
V7X_VMEM_BYTES = 64 * 1024 * 1024

ORIG_OFF_AQ = 0
ORIG_OFF_AK = ORIG_OFF_AQ + ATTN_WIDTH
ORIG_OFF_AV = ORIG_OFF_AK + KV_WIDTH
ORIG_OFF_CA = ORIG_OFF_AV + KV_WIDTH
ORIG_OFF_IF = ORIG_OFF_CA + 2 * CONV_WIDTH + 4 * MLSTM_WIDTH
ORIG_OFF_ZG = ORIG_OFF_IF + 2 * MLSTM_HEADS

OFF_AK = 0
OFF_CA = OFF_AK + KV_WIDTH
OFF_CG = OFF_CA + CONV_WIDTH
OFF_MQ = OFF_CG + CONV_WIDTH
OFF_MK = OFF_MQ + MLSTM_WIDTH
OFF_MV = OFF_MK + MLSTM_WIDTH
OFF_MO = OFF_MV + MLSTM_WIDTH
OFF_ZG = OFF_MO + MLSTM_WIDTH
OFF_IF = OFF_ZG + N_BRANCH * D_MODEL
N_IN_PACKED = OFF_IF + V7X_LANES

CONV_TAIL = 32
CONV_SHIFT_PAD = CONV_TAIL - V7X_SUBLANES
QK_ROW_BLOCKS = 4
CONV_LANE_BLOCK = 256
CONV_ROW_BLOCK = 64
CN_WIDTH = 2 * MLSTM_HEAD_DIM
QK_TAIL = 8

FFN_ROW_TILE = 512
MIXER_TIME_TILE = 384
FFN_VMEM_LIMIT = 56 * 1024 * 1024
MIXER_VMEM_LIMIT = 60 * 1024 * 1024

F32 = jnp.float32
BF16 = jnp.bfloat16


def _const_spec(shape):
    nd = len(shape)
    return pl.BlockSpec(shape, lambda *_: (0,) * nd, pipeline_mode=pl.Buffered(1))


def _dot(a, b):
    return jnp.dot(a, b, preferred_element_type=F32)


def _dot_nt(a, b):
    return lax.dot_general(a, b, (((1,), (1,)), ((), ())), preferred_element_type=F32)


def _dot_tn(a, b):
    return lax.dot_general(a, b, (((0,), (0,)), ((), ())), preferred_element_type=F32)


def _rms_rows(x, gain):
    ms = jnp.mean(x * x, axis=-1, keepdims=True)
    return x * lax.rsqrt(ms + RMS_EPS) * gain


def _seg_sumsq(x, ones_ref):
    x2 = x * x
    hi = x2.astype(BF16)
    lo = (x2 - hi.astype(F32)).astype(BF16)
    ones = ones_ref[...]
    return _dot(hi, ones) + _dot(lo, ones)


def _lane_bcast_heads(x, sel_ref):
    sel = sel_ref[...]
    hi = x.astype(BF16)
    r1 = x - hi.astype(F32)
    mid = r1.astype(BF16)
    lo = (r1 - mid.astype(F32)).astype(BF16)
    return _dot(hi, sel) + _dot(mid, sel) + _dot(lo, sel)


def _ffn_body(x_ref, g_ref, w1_ref, w3_ref, w2_ref, o_ref):
    x = x_ref[...]
    h = _rms_rows(x, g_ref[...]).astype(BF16)
    a = _dot(h, w1_ref[...])
    b = _dot(h, w3_ref[...])
    u = (a * jax.nn.sigmoid(a) * b).astype(BF16)
    o_ref[...] = x + 0.5 * _dot(u, w2_ref[...])


def _ffn(x2d, g, w1, w3, w2):
    rows = x2d.shape[0]
    tm = FFN_ROW_TILE
    assert rows % tm == 0
    return pl.pallas_call(
        _ffn_body,
        grid=(rows // tm,),
        in_specs=[
            pl.BlockSpec((tm, D_MODEL), lambda i: (i, 0)),
            _const_spec((1, D_MODEL)),
            _const_spec((D_MODEL, D_FF)),
            _const_spec((D_MODEL, D_FF)),
            _const_spec((D_FF, D_MODEL)),
        ],
        out_specs=pl.BlockSpec((tm, D_MODEL), lambda i: (i, 0)),
        out_shape=jax.ShapeDtypeStruct(x2d.shape, F32),
        compiler_params=pltpu.CompilerParams(
            dimension_semantics=("arbitrary",), vmem_limit_bytes=FFN_VMEM_LIMIT),
        name="ffn",
    )(x2d, g, w1, w3, w2)


def _mixer_body(x_ref, g_ref, win_ref, wqt_ref, wvt_ref, gb_ref, qg_ref, kg_ref, sink_ref,
                woa_ref, cw_ref, cb_ref, lng_ref, lnb_ref, woc_ref, qkw_ref, qkb_ref, ifb_ref,
                wom_ref, wout_ref, onesk_ref, band_ref, sel_ref,
                o_ref,
                kext, vext_t, uext, work_s, qkext, qm_s, km_s, vaug_s, ht_s, r_s, we_s, bb_s, rm_s,
                cnprev_s, mprev_s, ya_s, h_s, zg_s, pj_s, cacc_s, ycin_s, yo_s, cn_s, m_s,
                *, tt, n_pad):
    t_idx = pl.program_id(1)
    nblk = tt // BLOCK
    nchunk = tt // MLSTM_CHUNK

    @pl.when(t_idx == 0)
    def _init_state():
        kext[0:BLOCK, :] = jnp.zeros((BLOCK, KV_WIDTH), BF16)
        vext_t[:, 0:BLOCK] = jnp.zeros((KV_WIDTH, BLOCK), BF16)
        uext[0:CONV_TAIL, :] = jnp.zeros((CONV_TAIL, CONV_WIDTH), F32)
        qkext[0:QK_TAIL, :] = jnp.zeros((QK_TAIL, 2 * MLSTM_WIDTH), F32)
        cn_s[...] = jnp.zeros(cn_s.shape, F32)
        m_s[...] = jnp.full(m_s.shape, NEG, F32)

    t0 = t_idx * tt
    valid_col = (lax.broadcasted_iota(jnp.int32, (tt, 1), 0) + t0) >= n_pad

    x = x_ref[0]
    h = _rms_rows(x, g_ref[...]).astype(BF16)
    h_s[...] = h

    def proj(lo, hi):
        return _dot(h, win_ref[:, lo:hi])

    qt = _dot_nt(wqt_ref[...], h)
    vt = _dot_nt(wvt_ref[...], h)
    k = proj(OFF_AK, OFF_CA)
    k = k * lax.rsqrt(_seg_sumsq(k, onesk_ref) * (1.0 / HEAD_DIM) + RMS_EPS) * kg_ref[...]
    kext[BLOCK:BLOCK + tt, :] = k.astype(BF16)
    vext_t[:, BLOCK:BLOCK + tt] = vt.astype(BF16)

    qn = []
    for a in range(ATTN_HEADS):
        qa = qt[a * HEAD_DIM:(a + 1) * HEAD_DIM, :]
        inv = lax.rsqrt(jnp.mean(qa * qa, axis=0, keepdims=True) + RMS_EPS) * (HEAD_DIM ** -0.5)
        qn.append(qa * inv)
    qgain = qg_ref[...]
    band = band_ref[...]
    zeros_q = jnp.zeros((HEAD_DIM, ATTN_GROUP * BLOCK), BF16)
    key_iota = lax.broadcasted_iota(jnp.int32, (2 * BLOCK, ATTN_GROUP * BLOCK), 0)
    for n in range(nblk):
        r0 = n * BLOCK
        key_pos = key_iota + (t0 + r0 - BLOCK)
        bias = jnp.where(key_pos >= n_pad, band, NEG)
        kcat = kext[r0:r0 + 2 * BLOCK, :]
        for hk in range(ATTN_KV_HEADS):
            qh = jnp.concatenate(
                [(qn[hk * ATTN_GROUP + g][:, r0:r0 + BLOCK] * qgain).astype(BF16)
                 for g in range(ATTN_GROUP)], axis=1)
            rhs = jnp.concatenate([qh, zeros_q] if hk == 0 else [zeros_q, qh], axis=0)
            st = _dot(kcat, rhs) + bias
            sink = sink_ref[hk:hk + 1, :]
            m = jnp.maximum(jnp.max(st, axis=0, keepdims=True), sink)
            p = jnp.exp(st - m)
            den = jnp.sum(p, axis=0, keepdims=True) + jnp.exp(sink - m)
            vth = vext_t[hk * HEAD_DIM:(hk + 1) * HEAD_DIM, r0:r0 + 2 * BLOCK]
            ot = _dot(vth, p.astype(BF16)) / den
            for j in range(ATTN_GROUP // 2):
                pair = jnp.concatenate(
                    [ot[:, (2 * j) * BLOCK:(2 * j + 1) * BLOCK],
                     ot[:, (2 * j + 1) * BLOCK:(2 * j + 2) * BLOCK]], axis=0)
                c0 = (hk * ATTN_GROUP + 2 * j) * HEAD_DIM
                ya_s[r0:r0 + BLOCK, c0:c0 + 2 * HEAD_DIM] = pair.T.astype(BF16)
    kext[0:BLOCK, :] = kext[tt:tt + BLOCK, :]
    vext_t[:, 0:BLOCK] = vext_t[:, tt:tt + BLOCK]

    ca = proj(OFF_CA, OFF_CG)
    cg = proj(OFF_CG, OFF_MQ)
    u = jnp.where(valid_col, ca * jax.nn.sigmoid(cg), 0.0)
    uext[CONV_TAIL:CONV_TAIL + tt, :] = u
    sh_rows = tt + CONV_SHIFT_PAD
    for s in range(1, V7X_SUBLANES):
        work_s[(s - 1) * sh_rows:s * sh_rows, :] = uext[s:s + sh_rows, :]
    base = CONV_TAIL - (CONV_KERNEL - 1)
    n_rblk = tt // CONV_ROW_BLOCK
    pj_slab = 3 * MLSTM_WIDTH // n_rblk
    assert pj_slab % V7X_LANES == 0

    def conv_body(i, carry):
        z0 = pl.multiple_of(i * pj_slab, V7X_LANES)
        pj_s[:, pl.ds(z0, pj_slab)] = _dot(h_s[...], win_ref[:, pl.ds(OFF_MQ + z0, pj_slab)])
        r0 = pl.multiple_of(i * CONV_ROW_BLOCK, CONV_ROW_BLOCK)
        for ls in (slice(k * CONV_LANE_BLOCK, (k + 1) * CONV_LANE_BLOCK)
                   for k in range(CONV_WIDTH // CONV_LANE_BLOCK)):
            acc_blk = jnp.zeros((CONV_ROW_BLOCK, CONV_LANE_BLOCK), F32) + cb_ref[:, ls]
            for s in range(V7X_SUBLANES):
                offs = [o for o in range(base, base + CONV_KERNEL) if o % V7X_SUBLANES == s]
                a_lo = (min(offs) // V7X_SUBLANES) * V7X_SUBLANES
                n_win = (max(offs) // V7X_SUBLANES) * V7X_SUBLANES - a_lo + CONV_ROW_BLOCK
                if s == 0:
                    win = uext[pl.ds(r0 + a_lo, n_win), ls]
                else:
                    win = work_s[pl.ds(r0 + (s - 1) * sh_rows + a_lo, n_win), ls]
                for o in offs:
                    a = (o // V7X_SUBLANES) * V7X_SUBLANES - a_lo
                    acc_blk = acc_blk + win[a:a + CONV_ROW_BLOCK, :] * cw_ref[o - base:o - base + 1, ls]
            cacc_s[pl.ds(r0, CONV_ROW_BLOCK), ls] = acc_blk
        return carry

    lax.fori_loop(0, n_rblk, conv_body, 0)
    uext[0:CONV_TAIL, :] = uext[tt:tt + CONV_TAIL, :]
    acc = cacc_s[...]
    mu = jnp.mean(acc, axis=-1, keepdims=True)
    xc = acc - mu
    yn = xc * lax.rsqrt(jnp.mean(xc * xc, axis=-1, keepdims=True) + LN_EPS)
    yn = yn * lng_ref[...] + lnb_ref[...]
    ycin_s[...] = (yn * jax.nn.sigmoid(yn)).astype(BF16)

    qkext[QK_TAIL:QK_TAIL + tt, :] = jnp.where(valid_col, pj_s[:, :2 * MLSTM_WIDTH], 0.0)
    qbase = QK_TAIL - (QK_CONV_KERNEL - 1)
    qrb = tt // QK_ROW_BLOCKS
    oslab = D_MODEL // QK_ROW_BLOCKS
    assert qrb % V7X_SUBLANES == 0 and oslab % V7X_LANES == 0

    def qk_body(i, carry):
        r0 = pl.multiple_of(i * qrb, V7X_SUBLANES)
        for k in range(2 * MLSTM_WIDTH // CONV_LANE_BLOCK):
            ls = slice(k * CONV_LANE_BLOCK, (k + 1) * CONV_LANE_BLOCK)
            win = qkext[pl.ds(r0, qrb + QK_TAIL), ls]
            qacc = jnp.zeros((qrb, CONV_LANE_BLOCK), F32) + qkb_ref[:, ls]
            for j in range(QK_CONV_KERNEL):
                qacc = qacc + win[qbase + j:qbase + j + qrb, :] * qkw_ref[j:j + 1, ls]
            y = qacc * jax.nn.sigmoid(qacc)
            if k < MLSTM_WIDTH // CONV_LANE_BLOCK:
                qm_s[pl.ds(r0, qrb), ls] = y * (MLSTM_HEAD_DIM ** -0.5)
            else:
                km_s[pl.ds(r0, qrb), k * CONV_LANE_BLOCK - MLSTM_WIDTH:(k + 1) * CONV_LANE_BLOCK - MLSTM_WIDTH] = y
        z0 = pl.multiple_of(i * oslab, V7X_LANES)
        yo_s[:, pl.ds(D_MODEL + z0, oslab)] = _dot(ycin_s[...], woc_ref[:, pl.ds(z0, oslab)])
        return carry

    lax.fori_loop(0, QK_ROW_BLOCKS, qk_body, 0)
    qkext[0:QK_TAIL, :] = qkext[tt:tt + QK_TAIL, :]
    ya = _dot(ya_s[...], woa_ref[...])
    yc = yo_s[:, D_MODEL:]
    vm = pj_s[:, 2 * MLSTM_WIDTH:]
    ones_v = jnp.ones((tt, MLSTM_HEAD_DIM), BF16)
    for hd in range(MLSTM_HEADS):
        vaug_s[:, hd * CN_WIDTH:hd * CN_WIDTH + MLSTM_HEAD_DIM] = (
            vm[:, hd * MLSTM_HEAD_DIM:(hd + 1) * MLSTM_HEAD_DIM].astype(BF16))
        vaug_s[:, hd * CN_WIDTH + MLSTM_HEAD_DIM:(hd + 1) * CN_WIDTH] = ones_v

    zif = proj(OFF_IF, N_IN_PACKED) + ifb_ref[...]
    log_i = jnp.where(valid_col, zif, NEG)
    log_f = pltpu.roll(jax.nn.log_sigmoid(zif), V7X_LANES - MLSTM_HEADS, axis=1)
    pos = lax.broadcasted_iota(jnp.int32, (tt, 1), 0) % MLSTM_CHUNK
    bcum = log_f
    d = 1
    while d < MLSTM_CHUNK:
        bcum = bcum + jnp.where(pos >= d, pltpu.roll(bcum, d, axis=0), 0.0)
        d *= 2
    r = log_i - bcum
    rmax = r
    d = 1
    while d < MLSTM_CHUNK:
        rmax = jnp.where(pos >= d, jnp.maximum(rmax, pltpu.roll(rmax, d, axis=0)), rmax)
        d *= 2
    rmax3 = rmax.reshape(nchunk, MLSTM_CHUNK, V7X_LANES)
    rmax_end = jnp.broadcast_to(rmax3[:, MLSTM_CHUNK - 1:MLSTM_CHUNK, :], rmax3.shape).reshape(tt, V7X_LANES)
    r_s[...] = r
    we_s[...] = jnp.exp(r - rmax_end)
    bb_s[...] = _lane_bcast_heads(bcum, sel_ref)
    rm_s[...] = _lane_bcast_heads(rmax, sel_ref)

    npair = tt // (2 * MLSTM_CHUNK)
    pair = 2 * MLSTM_CHUNK
    zg_slab = N_BRANCH * D_MODEL // (2 * npair)
    assert zg_slab % V7X_LANES == 0
    nd_base = nchunk * (MLSTM_HEADS // 2) * MLSTM_HEAD_DIM
    assert nd_base + (MLSTM_HEADS // 2) * tt <= work_s.shape[0]
    ri = lax.broadcasted_iota(jnp.int32, (pair, pair), 0)
    ci = lax.broadcasted_iota(jnp.int32, (pair, pair), 1)
    mask2 = (ri >= ci) & ((ri // MLSTM_CHUNK) == (ci // MLSTM_CHUNK))
    first_half = lax.broadcasted_iota(jnp.int32, (1, pair), 1) < MLSTM_CHUNK

    def zg_project(slab_idx):
        z0 = pl.multiple_of(slab_idx * zg_slab, V7X_LANES)
        zg_s[:, pl.ds(z0, zg_slab)] = _dot(h_s[...], win_ref[:, pl.ds(OFF_ZG + z0, zg_slab)])

    def local_body(p, carry):
        zg_project(p)
        r0 = pl.multiple_of(p * pair, pair)
        rows = pl.ds(r0, pair)
        rt = r_s[rows, :].T
        wet = we_s[rows, :].T
        for hd in range(MLSTM_HEADS):
            lanes = slice(hd * MLSTM_HEAD_DIM, (hd + 1) * MLSTM_HEAD_DIM)
            qb = qm_s[rows, lanes].astype(BF16)
            kt = km_s[rows, lanes].T
            vaug = vaug_s[rows, hd * CN_WIDTH:(hd + 1) * CN_WIDTH]
            bbp = bb_s[rows, lanes]
            log_d = jnp.where(mask2, bbp + rt[hd:hd + 1, :], NEG)
            pmat = jnp.exp(jnp.minimum(log_d - (bbp + rm_s[rows, lanes]), 0.0))
            s = _dot(qb, kt.astype(BF16)) * pmat
            kw = kt * wet[hd:hd + 1, :]
            zero = jnp.zeros_like(kw)
            lhs = jnp.concatenate(
                [s, jnp.where(first_half, kw, zero), jnp.where(first_half, zero, kw)], axis=0)
            out = _dot(lhs.astype(BF16), vaug)
            half = slice((hd % 2) * CN_WIDTH, (hd % 2 + 1) * CN_WIDTH)
            work_s[pl.ds(nd_base + (hd // 2) * tt + r0, pair), half] = out[:pair]
            for j in range(2):
                slot = pl.multiple_of(((2 * p + j) * (MLSTM_HEADS // 2) + hd // 2) * MLSTM_HEAD_DIM,
                                      MLSTM_HEAD_DIM)
                work_s[pl.ds(slot, MLSTM_HEAD_DIM), half] = out[(1 + j) * pair:(2 + j) * pair]
        return carry

    lax.fori_loop(0, npair, local_body, 0)

    for hd in range(MLSTM_HEADS):
        lanes = slice(hd * MLSTM_HEAD_DIM, (hd + 1) * MLSTM_HEAD_DIM)
        cn = cn_s[hd]
        m_prev = m_s[hd:hd + 1, :]
        for c in range(nchunk):
            e = (c + 1) * MLSTM_CHUNK - 1
            cnprev_s[c * MLSTM_HEADS + hd] = cn.astype(BF16)
            mprev_s[c * MLSTM_HEADS + hd] = jnp.broadcast_to(m_prev, (V7X_SUBLANES, V7X_LANES))
            b_end = bb_s[e:e + 1, lanes]
            m_loc = b_end + rm_s[e:e + 1, lanes]
            m_new = jnp.maximum(b_end + m_prev, m_loc)
            a = jnp.exp(b_end + m_prev - m_new)
            cc = jnp.exp(m_loc - m_new)
            slot = (c * (MLSTM_HEADS // 2) + hd // 2) * MLSTM_HEAD_DIM
            inc = work_s[slot:slot + MLSTM_HEAD_DIM, (hd % 2) * CN_WIDTH:(hd % 2 + 1) * CN_WIDTH]
            cn = jnp.concatenate([a, a], axis=1) * cn + jnp.concatenate([cc, cc], axis=1) * inc
            m_prev = m_new
        cn_s[hd] = cn
        m_s[hd:hd + 1, :] = m_prev

    def output_body(p, carry):
        zg_project(npair + p)
        for j in range(2):
            c = 2 * p + j
            c0 = pl.multiple_of(c * MLSTM_CHUNK, MLSTM_CHUNK)
            rows = pl.ds(c0, MLSTM_CHUNK)
            for hd in range(MLSTM_HEADS):
                lanes = slice(hd * MLSTM_HEAD_DIM, (hd + 1) * MLSTM_HEAD_DIM)
                inter = _dot(qm_s[rows, lanes].astype(BF16), cnprev_s[c * MLSTM_HEADS + hd])
                local = work_s[pl.ds(nd_base + (hd // 2) * tt + c0, MLSTM_CHUNK),
                               (hd % 2) * CN_WIDTH:(hd % 2 + 1) * CN_WIDTH]
                bbc = bb_s[rows, lanes]
                m_intra = bbc + rm_s[rows, lanes]
                m_inter = bbc + mprev_s[c * MLSTM_HEADS + hd][0:1, :]
                m_t = jnp.maximum(m_inter, m_intra)
                e_intra = jnp.exp(m_intra - m_t)
                w_inter = jnp.exp(m_inter - m_t)
                num = local[:, :MLSTM_HEAD_DIM] * e_intra + w_inter * inter[:, :MLSTM_HEAD_DIM]
                den = local[:, MLSTM_HEAD_DIM:] * e_intra + w_inter * inter[:, MLSTM_HEAD_DIM:]
                ht_s[rows, lanes] = num / jnp.maximum(jnp.abs(den), jnp.exp(-m_t))
        return carry

    lax.fori_loop(0, npair, output_body, 0)

    mo = proj(OFF_MO, OFF_ZG)
    ym = _dot((jax.nn.sigmoid(mo) * ht_s[...]).astype(BF16), wom_ref[...])

    def gate(i, y):
        zg = zg_s[:, i * D_MODEL:(i + 1) * D_MODEL] + gb_ref[i:i + 1, :]
        return jax.nn.sigmoid(zg) * y

    y = gate(0, ya) + gate(1, yc) + gate(2, ym)
    o_ref[0] = x + _dot(y.astype(BF16), wout_ref[...])


def _mixer(hseq, p, consts, *, n_pad):
    bsz, t_len, _ = hseq.shape
    tt = MIXER_TIME_TILE
    assert t_len % tt == 0 and tt % BLOCK == 0 and tt % MLSTM_CHUNK == 0
    assert tt >= CONV_TAIL and tt >= BLOCK
    nchunk = tt // MLSTM_CHUNK
    assert tt % (2 * MLSTM_CHUNK) == 0
    body = functools.partial(_mixer_body, tt=tt, n_pad=n_pad)
    in_specs = [
        pl.BlockSpec((1, tt, D_MODEL), lambda b, t: (b, t, 0)),
        _const_spec((1, D_MODEL)),
        _const_spec((D_MODEL, N_IN_PACKED)),
        _const_spec((ATTN_WIDTH, D_MODEL)),
        _const_spec((KV_WIDTH, D_MODEL)),
        _const_spec((N_BRANCH, D_MODEL)),
        _const_spec((HEAD_DIM, BLOCK)),
        _const_spec((1, KV_WIDTH)),
        _const_spec((ATTN_KV_HEADS, ATTN_GROUP * BLOCK)),
        _const_spec((ATTN_WIDTH, D_MODEL)),
        _const_spec((CONV_KERNEL, CONV_WIDTH)),
        _const_spec((1, CONV_WIDTH)),
        _const_spec((1, CONV_WIDTH)),
        _const_spec((1, CONV_WIDTH)),
        _const_spec((CONV_WIDTH, D_MODEL)),
        _const_spec((QK_CONV_KERNEL, 2 * MLSTM_WIDTH)),
        _const_spec((1, 2 * MLSTM_WIDTH)),
        _const_spec((1, V7X_LANES)),
        _const_spec((MLSTM_WIDTH, D_MODEL)),
        _const_spec((D_MODEL, D_MODEL)),
        _const_spec((KV_WIDTH, KV_WIDTH)),
        _const_spec((2 * BLOCK, ATTN_GROUP * BLOCK)),
        _const_spec((V7X_LANES, MLSTM_WIDTH)),
    ]
    scratch = [
        pltpu.VMEM((BLOCK + tt, KV_WIDTH), BF16),
        pltpu.VMEM((KV_WIDTH, BLOCK + tt), BF16),
        pltpu.VMEM((CONV_TAIL + tt, CONV_WIDTH), F32),
        pltpu.VMEM(((V7X_SUBLANES - 1) * (tt + CONV_SHIFT_PAD), CONV_WIDTH), F32),
        pltpu.VMEM((QK_TAIL + tt, 2 * MLSTM_WIDTH), F32),
        pltpu.VMEM((tt, MLSTM_WIDTH), F32),
        pltpu.VMEM((tt, MLSTM_WIDTH), F32),
        pltpu.VMEM((tt, MLSTM_HEADS * CN_WIDTH), BF16),
        pltpu.VMEM((tt, MLSTM_WIDTH), F32),
        pltpu.VMEM((tt, V7X_LANES), F32),
        pltpu.VMEM((tt, V7X_LANES), F32),
        pltpu.VMEM((tt, MLSTM_WIDTH), F32),
        pltpu.VMEM((tt, MLSTM_WIDTH), F32),
        pltpu.VMEM((nchunk * MLSTM_HEADS, MLSTM_HEAD_DIM, CN_WIDTH), BF16),
        pltpu.VMEM((nchunk * MLSTM_HEADS, V7X_SUBLANES, V7X_LANES), F32),
        pltpu.VMEM((tt, ATTN_WIDTH), BF16),
        pltpu.VMEM((tt, D_MODEL), BF16),
        pltpu.VMEM((tt, N_BRANCH * D_MODEL), F32),
        pltpu.VMEM((tt, 3 * MLSTM_WIDTH), F32),
        pltpu.VMEM((tt, CONV_WIDTH), F32),
        pltpu.VMEM((tt, CONV_WIDTH), BF16),
        pltpu.VMEM((tt, 2 * D_MODEL), F32),
        pltpu.VMEM((MLSTM_HEADS, MLSTM_HEAD_DIM, CN_WIDTH), F32),
        pltpu.VMEM((V7X_SUBLANES, V7X_LANES), F32),
    ]
    return pl.pallas_call(
        body,
        grid=(bsz, t_len // tt),
        in_specs=in_specs,
        out_specs=pl.BlockSpec((1, tt, D_MODEL), lambda b, t: (b, t, 0)),
        out_shape=jax.ShapeDtypeStruct(hseq.shape, F32),
        scratch_shapes=scratch,
        compiler_params=pltpu.CompilerParams(
            dimension_semantics=("arbitrary", "arbitrary"),
            vmem_limit_bytes=MIXER_VMEM_LIMIT),
        name="mixer",
    )(hseq, p["norm"], p["w_in"], p["w_q_t"], p["w_v_t"], p["gate_bias"], p["q_gain"],
      p["k_gain"], p["sinks"], p["w_o_attn"], p["conv_w"], p["conv_b"], p["ln_g"], p["ln_b"],
      p["w_o_conv"], p["qk_w"], p["qk_b"], p["if_bias"], p["w_o_mlstm"], p["w_out"],
      consts["ones_k"], consts["band"], consts["sel"])


def _make_consts():
    i = jnp.arange(KV_WIDTH) // HEAD_DIM
    kj = jnp.arange(2 * BLOCK)
    qi = jnp.arange(ATTN_GROUP * BLOCK) % BLOCK
    allowed = (kj[:, None] > qi[None, :]) & (kj[:, None] <= qi[None, :] + BLOCK)
    lane = jnp.arange(V7X_LANES)
    col_head = jnp.arange(MLSTM_WIDTH) // MLSTM_HEAD_DIM
    return {
        "sel": (lane[:, None] == col_head[None, :]).astype(BF16),
        "ones_k": (i[:, None] == i[None, :]).astype(BF16),
        "band": jnp.where(allowed, 0.0, NEG).astype(F32),
    }


def _pack_w_in(w_in):
    pad = jnp.zeros((D_MODEL, V7X_LANES - 2 * MLSTM_HEADS), w_in.dtype)
    return jnp.concatenate(
        [w_in[:, ORIG_OFF_AK:ORIG_OFF_AV], w_in[:, ORIG_OFF_CA:ORIG_OFF_IF], w_in[:, ORIG_OFF_ZG:],
         w_in[:, ORIG_OFF_IF:ORIG_OFF_ZG], pad], axis=1).astype(BF16)


def kernel(x, meta_tokens, ffn1_norm, ffn1_w1, ffn1_w3, ffn1_w2, mix_norm, w_in, gate_bias,
           attn_q_norm, attn_k_norm, attn_sinks, w_o_attn, conv_dw_w, conv_dw_b, conv_ln_g,
           conv_ln_b, w_o_conv, mlstm_qk_conv_w, mlstm_qk_conv_b, mlstm_igate_bias,
           mlstm_fgate_bias, w_o_mlstm, w_out, ffn2_norm, ffn2_w1, ffn2_w3, ffn2_w2):
    bsz, seq, _ = x.shape
    depth = w_in.shape[0]
    lead = jnp.concatenate([
        jnp.zeros((bsz, N_PAD, D_MODEL), x.dtype),
        jnp.broadcast_to(meta_tokens.astype(x.dtype)[None], (bsz, N_META, D_MODEL))], axis=1)
    h = jnp.concatenate([lead, x], axis=1)
    t_len = h.shape[1]
    consts = _make_consts()

    def row(v):
        return v.reshape(1, -1).astype(F32)

    for l in range(depth):
        if_bias = jnp.concatenate([
            mlstm_igate_bias[l], mlstm_fgate_bias[l],
            jnp.zeros((V7X_LANES - 2 * MLSTM_HEADS,), F32)]).reshape(1, V7X_LANES)
        mix = {
            "norm": row(mix_norm[l]),
            "w_in": _pack_w_in(w_in[l]),
            "w_q_t": w_in[l][:, ORIG_OFF_AQ:ORIG_OFF_AK].T.astype(BF16),
            "w_v_t": w_in[l][:, ORIG_OFF_AV:ORIG_OFF_CA].T.astype(BF16),
            "gate_bias": gate_bias[l].astype(F32),
            "q_gain": jnp.broadcast_to(attn_q_norm[l].astype(F32)[:, None], (HEAD_DIM, BLOCK)),
            "k_gain": row(jnp.tile(attn_k_norm[l], ATTN_KV_HEADS)),
            "sinks": jnp.repeat(attn_sinks[l].astype(F32), BLOCK).reshape(
                ATTN_KV_HEADS, ATTN_GROUP * BLOCK),
            "w_o_attn": w_o_attn[l].astype(BF16),
            "conv_w": conv_dw_w[l].astype(F32),
            "conv_b": row(conv_dw_b[l]),
            "ln_g": row(conv_ln_g[l]),
            "ln_b": row(conv_ln_b[l]),
            "w_o_conv": w_o_conv[l].astype(BF16),
            "qk_w": mlstm_qk_conv_w[l].astype(F32),
            "qk_b": row(mlstm_qk_conv_b[l]),
            "if_bias": if_bias,
            "w_o_mlstm": w_o_mlstm[l].astype(BF16),
            "w_out": w_out[l].astype(BF16),
        }
        h2 = h.reshape(bsz * t_len, D_MODEL)
        h2 = _ffn(h2, row(ffn1_norm[l]), ffn1_w1[l].astype(BF16), ffn1_w3[l].astype(BF16),
                  ffn1_w2[l].astype(BF16))
        h = _mixer(h2.reshape(bsz, t_len, D_MODEL), mix, consts, n_pad=N_PAD)
        h2 = h.reshape(bsz * t_len, D_MODEL)
        h2 = _ffn(h2, row(ffn2_norm[l]), ffn2_w1[l].astype(BF16), ffn2_w3[l].astype(BF16),
                  ffn2_w2[l].astype(BF16))
        h = h2.reshape(bsz, t_len, D_MODEL)
    return h[:, LEAD:]
```

```python
import functools

import jax
import jax.numpy as jnp
from jax import lax
from jax.experimental import pallas as pl
from jax.experimental.pallas import tpu as pltpu

D_MODEL = 1024
N_META = 16
BLOCK = 128
LEAD = BLOCK
N_PAD = LEAD - N_META
ATTN_HEADS = 8
ATTN_KV_HEADS = 2
HEAD_DIM = 64
ATTN_GROUP = ATTN_HEADS // ATTN_KV_HEADS
ATTN_WIDTH = ATTN_HEADS * HEAD_DIM
KV_WIDTH = ATTN_KV_HEADS * HEAD_DIM
CONV_WIDTH = 512
CONV_KERNEL = 31
MLSTM_HEADS = 4
MLSTM_WIDTH = 512
MLSTM_HEAD_DIM = MLSTM_WIDTH // MLSTM_HEADS
MLSTM_CHUNK = 64
QK_CONV_KERNEL = 4
D_FF = 2816
N_BRANCH = 3
RMS_EPS = 1e-6
LN_EPS = 1e-5
NEG = -1e30

V7X_LANES = 128
V7X_SUBLANES = 8

ORIG_OFF_AQ = 0
ORIG_OFF_AK = ORIG_OFF_AQ + ATTN_WIDTH
ORIG_OFF_AV = ORIG_OFF_AK + KV_WIDTH
ORIG_OFF_CA = ORIG_OFF_AV + KV_WIDTH
ORIG_OFF_IF = ORIG_OFF_CA + 2 * CONV_WIDTH + 4 * MLSTM_WIDTH
ORIG_OFF_ZG = ORIG_OFF_IF + 2 * MLSTM_HEADS

OFF_AK = 0
OFF_CA = OFF_AK + KV_WIDTH
OFF_CG = OFF_CA + CONV_WIDTH
OFF_MQ = OFF_CG + CONV_WIDTH
OFF_MK = OFF_MQ + MLSTM_WIDTH
OFF_MV = OFF_MK + MLSTM_WIDTH
OFF_MO = OFF_MV + MLSTM_WIDTH
OFF_ZG = OFF_MO + MLSTM_WIDTH
OFF_IF = OFF_ZG + N_BRANCH * D_MODEL
N_IN_PACKED = OFF_IF + V7X_LANES

CONV_TAIL = 32
CONV_SHIFT_PAD = CONV_TAIL - V7X_SUBLANES
QK_TAIL = 8

FFN_ROW_TILE = 512
MIXER_TIME_TILE = 512
FFN_VMEM_LIMIT = 56 * 1024 * 1024
MIXER_VMEM_LIMIT = 60 * 1024 * 1024

F32 = jnp.float32
BF16 = jnp.bfloat16

STATE_SHAPES = (
    ((BLOCK, KV_WIDTH), BF16),
    ((KV_WIDTH, BLOCK), BF16),
    ((CONV_TAIL, CONV_WIDTH), F32),
    ((QK_TAIL, 2 * MLSTM_WIDTH), F32),
    ((MLSTM_HEADS, MLSTM_HEAD_DIM, MLSTM_HEAD_DIM), F32),
    ((V7X_SUBLANES, MLSTM_HEAD_DIM), F32),
    ((V7X_SUBLANES, V7X_LANES), F32),
)
N_MIXER_PARAMS = 22


def _const_spec(shape):
    nd = len(shape)
    return pl.BlockSpec(shape, lambda *_: (0,) * nd, pipeline_mode=pl.Buffered(1))


def _dot(a, b):
    return jnp.dot(a, b, preferred_element_type=F32)


def _dot_nt(a, b):
    return lax.dot_general(a, b, (((1,), (1,)), ((), ())), preferred_element_type=F32)


def _dot_tn(a, b):
    return lax.dot_general(a, b, (((0,), (0,)), ((), ())), preferred_element_type=F32)


def _rms_rows(x, gain):
    ms = jnp.mean(x * x, axis=-1, keepdims=True)
    return x * lax.rsqrt(ms + RMS_EPS) * gain


def _seg_sumsq(x, ones_ref):
    x2 = x * x
    hi = x2.astype(BF16)
    lo = (x2 - hi.astype(F32)).astype(BF16)
    ones = ones_ref[...]
    return _dot(hi, ones) + _dot(lo, ones)


def _ffn_body(x_ref, g_ref, w1_ref, w3_ref, w2_ref, o_ref):
    x = x_ref[...]
    h = _rms_rows(x, g_ref[...]).astype(BF16)
    a = _dot(h, w1_ref[...])
    b = _dot(h, w3_ref[...])
    u = (a * jax.nn.sigmoid(a) * b).astype(BF16)
    o_ref[...] = x + 0.5 * _dot(u, w2_ref[...])


def _ffn(x2d, g, w1, w3, w2):
    rows = x2d.shape[0]
    tm = min(FFN_ROW_TILE, rows)
    assert rows % tm == 0
    return pl.pallas_call(
        _ffn_body,
        grid=(rows // tm,),
        in_specs=[
            pl.BlockSpec((tm, D_MODEL), lambda i: (i, 0)),
            _const_spec((1, D_MODEL)),
            _const_spec((D_MODEL, D_FF)),
            _const_spec((D_MODEL, D_FF)),
            _const_spec((D_FF, D_MODEL)),
        ],
        out_specs=pl.BlockSpec((tm, D_MODEL), lambda i: (i, 0)),
        out_shape=jax.ShapeDtypeStruct(x2d.shape, F32),
        compiler_params=pltpu.CompilerParams(
            dimension_semantics=("arbitrary",), vmem_limit_bytes=FFN_VMEM_LIMIT),
        name="ffn",
    )(x2d, g, w1, w3, w2)


def _mixer_body(*refs, tt, n_pad, pos0, has_init, emit_state):
    n_state = len(STATE_SHAPES)
    n_in = N_MIXER_PARAMS + (n_state if has_init else 0)
    (x_ref, g_ref, win_ref, wqt_ref, wvt_ref, gb_ref, qg_ref, kg_ref, sink_ref,
     woa_ref, cw_ref, cb_ref, lng_ref, lnb_ref, woc_ref, qkw_ref, qkb_ref, ifb_ref,
     wom_ref, wout_ref, onesk_ref, band_ref) = refs[:N_MIXER_PARAMS]
    init_refs = refs[N_MIXER_PARAMS:n_in]
    o_ref = refs[n_in]
    n_out = 1 + (n_state if emit_state else 0)
    final_refs = refs[n_in + 1:n_in + n_out]
    (kext, vext_t, uext, ush, qkext, qm_s, km_s, vm_s, ht_s, bcum_s, r_s, ya_s,
     c_s, n_s, m_s) = refs[n_in + n_out:]

    t_idx = pl.program_id(1)
    nblk = tt // BLOCK
    nchunk = tt // MLSTM_CHUNK
    mask_rows = pos0 < n_pad

    @pl.when(t_idx == 0)
    def _init_state():
        if has_init:
            k0, v0, u0, qk0, c0, n0, m0 = init_refs
            kext[0:BLOCK, :] = k0[...]
            vext_t[:, 0:BLOCK] = v0[...]
            uext[0:CONV_TAIL, :] = u0[...]
            qkext[0:QK_TAIL, :] = qk0[...]
            c_s[...] = c0[...]
            n_s[...] = n0[...]
            m_s[...] = m0[...]
        else:
            kext[0:BLOCK, :] = jnp.zeros((BLOCK, KV_WIDTH), BF16)
            vext_t[:, 0:BLOCK] = jnp.zeros((KV_WIDTH, BLOCK), BF16)
            uext[0:CONV_TAIL, :] = jnp.zeros((CONV_TAIL, CONV_WIDTH), F32)
            qkext[0:QK_TAIL, :] = jnp.zeros((QK_TAIL, 2 * MLSTM_WIDTH), F32)
            c_s[...] = jnp.zeros(c_s.shape, F32)
            n_s[...] = jnp.zeros(n_s.shape, F32)
            m_s[...] = jnp.full(m_s.shape, NEG, F32)

    t0 = t_idx * tt
    if mask_rows:
        valid_col = (lax.broadcasted_iota(jnp.int32, (tt, 1), 0) + (pos0 + t0)) >= n_pad

    def mask_pad_rows(v, fill):
        return jnp.where(valid_col, v, fill) if mask_rows else v

    x = x_ref[0]
    h = _rms_rows(x, g_ref[...]).astype(BF16)

    def proj(lo, hi):
        return _dot(h, win_ref[:, lo:hi])

    qt = _dot_nt(wqt_ref[...], h)
    vt = _dot_nt(wvt_ref[...], h)
    k = proj(OFF_AK, OFF_CA)
    k = k * lax.rsqrt(_seg_sumsq(k, onesk_ref) * (1.0 / HEAD_DIM) + RMS_EPS) * kg_ref[...]
    kext[BLOCK:BLOCK + tt, :] = k.astype(BF16)
    vext_t[:, BLOCK:BLOCK + tt] = vt.astype(BF16)

    qn = []
    for a in range(ATTN_HEADS):
        qa = qt[a * HEAD_DIM:(a + 1) * HEAD_DIM, :]
        inv = lax.rsqrt(jnp.mean(qa * qa, axis=0, keepdims=True) + RMS_EPS) * (HEAD_DIM ** -0.5)
        qn.append(qa * inv)
    qgain = qg_ref[...]
    band = band_ref[...]
    zeros_q = jnp.zeros((HEAD_DIM, ATTN_GROUP * BLOCK), BF16)
    key_iota = lax.broadcasted_iota(jnp.int32, (2 * BLOCK, ATTN_GROUP * BLOCK), 0)
    for n in range(nblk):
        r0 = n * BLOCK
        if pos0 + r0 - BLOCK >= n_pad:
            bias = band
        else:
            key_pos = key_iota + (pos0 + t0 + r0 - BLOCK)
            bias = jnp.where(key_pos >= n_pad, band, NEG)
        kcat = kext[r0:r0 + 2 * BLOCK, :]
        for hk in range(ATTN_KV_HEADS):
            qh = jnp.concatenate(
                [(qn[hk * ATTN_GROUP + g][:, r0:r0 + BLOCK] * qgain).astype(BF16)
                 for g in range(ATTN_GROUP)], axis=1)
            rhs = jnp.concatenate([qh, zeros_q] if hk == 0 else [zeros_q, qh], axis=0)
            st = _dot(kcat, rhs) + bias
            sink = sink_ref[hk:hk + 1, :]
            m = jnp.maximum(jnp.max(st, axis=0, keepdims=True), sink)
            p = jnp.exp(st - m)
            den = jnp.sum(p, axis=0, keepdims=True) + jnp.exp(sink - m)
            vth = vext_t[hk * HEAD_DIM:(hk + 1) * HEAD_DIM, r0:r0 + 2 * BLOCK]
            ot = _dot(vth, p.astype(BF16)) / den
            for j in range(ATTN_GROUP // 2):
                pair = jnp.concatenate(
                    [ot[:, (2 * j) * BLOCK:(2 * j + 1) * BLOCK],
                     ot[:, (2 * j + 1) * BLOCK:(2 * j + 2) * BLOCK]], axis=0)
                c0 = (hk * ATTN_GROUP + 2 * j) * HEAD_DIM
                ya_s[r0:r0 + BLOCK, c0:c0 + 2 * HEAD_DIM] = pair.T.astype(BF16)
    kext[0:BLOCK, :] = kext[tt:tt + BLOCK, :]
    vext_t[:, 0:BLOCK] = vext_t[:, tt:tt + BLOCK]
    ya = _dot(ya_s[...], woa_ref[...])

    ca = proj(OFF_CA, OFF_CG)
    cg = proj(OFF_CG, OFF_MQ)
    uext[CONV_TAIL:CONV_TAIL + tt, :] = mask_pad_rows(ca * jax.nn.sigmoid(cg), 0.0)
    for s in range(1, V7X_SUBLANES):
        ush[s - 1] = uext[s:s + tt + CONV_SHIFT_PAD, :]
    base = CONV_TAIL - (CONV_KERNEL - 1)
    acc = jnp.zeros((tt, CONV_WIDTH), F32) + cb_ref[...]
    for j in range(CONV_KERNEL):
        off = base + j
        s, a = off % V7X_SUBLANES, (off // V7X_SUBLANES) * V7X_SUBLANES
        tap = uext[a:a + tt, :] if s == 0 else ush[s - 1, a:a + tt, :]
        acc = acc + tap * cw_ref[j:j + 1, :]
    uext[0:CONV_TAIL, :] = uext[tt:tt + CONV_TAIL, :]
    mu = jnp.mean(acc, axis=-1, keepdims=True)
    xc = acc - mu
    yn = xc * lax.rsqrt(jnp.mean(xc * xc, axis=-1, keepdims=True) + LN_EPS)
    yn = yn * lng_ref[...] + lnb_ref[...]
    yc = _dot((yn * jax.nn.sigmoid(yn)).astype(BF16), woc_ref[...])

    mqk = proj(OFF_MQ, OFF_MV)
    qkext[QK_TAIL:QK_TAIL + tt, :] = mask_pad_rows(mqk, 0.0)
    qbase = QK_TAIL - (QK_CONV_KERNEL - 1)
    qacc = jnp.zeros((tt, 2 * MLSTM_WIDTH), F32) + qkb_ref[...]
    for j in range(QK_CONV_KERNEL):
        qacc = qacc + qkext[qbase + j:qbase + j + tt, :] * qkw_ref[j:j + 1, :]
    qkext[0:QK_TAIL, :] = qkext[tt:tt + QK_TAIL, :]
    qk = qacc * jax.nn.sigmoid(qacc)
    qm_s[...] = qk[:, :MLSTM_WIDTH] * (MLSTM_HEAD_DIM ** -0.5)
    km_s[...] = qk[:, MLSTM_WIDTH:]
    vm_s[...] = proj(OFF_MV, OFF_MO)

    zif = proj(OFF_IF, N_IN_PACKED) + ifb_ref[...]
    log_i = mask_pad_rows(zif, NEG)
    log_f = pltpu.roll(jax.nn.log_sigmoid(zif), V7X_LANES - MLSTM_HEADS, axis=1)
    pos = lax.broadcasted_iota(jnp.int32, (tt, 1), 0) % MLSTM_CHUNK
    bcum = log_f
    d = 1
    while d < MLSTM_CHUNK:
        bcum = bcum + jnp.where(pos >= d, pltpu.roll(bcum, d, axis=0), 0.0)
        d *= 2
    bcum_s[...] = bcum
    r_s[...] = log_i - bcum

    tri = (lax.broadcasted_iota(jnp.int32, (MLSTM_CHUNK, MLSTM_CHUNK), 0) >=
           lax.broadcasted_iota(jnp.int32, (MLSTM_CHUNK, MLSTM_CHUNK), 1))

    def chunk_body(c, carry):
        r0 = pl.multiple_of(c * MLSTM_CHUNK, MLSTM_CHUNK)
        rows = pl.ds(r0, MLSTM_CHUNK)
        bc = bcum_s[rows, :]
        rc = r_s[rows, :]
        rct = rc.T
        for hd in range(MLSTM_HEADS):
            lanes = slice(hd * MLSTM_HEAD_DIM, (hd + 1) * MLSTM_HEAD_DIM)
            qc = qm_s[rows, lanes]
            kc = km_s[rows, lanes]
            vc = vm_s[rows, lanes]
            bcol = bc[:, hd:hd + 1]
            rcol = rc[:, hd:hd + 1]
            rrow = rct[hd:hd + 1, :]
            b_end = bc[MLSTM_CHUNK - 1:MLSTM_CHUNK, hd:hd + 1]
            m_prev = m_s[hd:hd + 1, 0:1]
            n_prev = n_s[hd:hd + 1, :]
            c_prev = c_s[hd]

            log_d = jnp.where(tri, bcol + rrow, NEG)
            m_inter = bcol + m_prev
            m_t = jnp.maximum(m_inter, jnp.max(log_d, axis=-1, keepdims=True))
            w_inter = jnp.exp(m_inter - m_t)
            qb = qc.astype(BF16)
            kb = kc.astype(BF16)
            vb = vc.astype(BF16)
            s = _dot_nt(qb, kb) * jnp.exp(log_d - m_t)
            num = _dot(s.astype(BF16), vb) + w_inter * _dot(qb, c_prev.astype(BF16))
            den = (jnp.sum(s, axis=-1, keepdims=True) +
                   w_inter * jnp.sum(qc * n_prev, axis=-1, keepdims=True))
            ht_s[rows, lanes] = num / jnp.maximum(jnp.abs(den), jnp.exp(-m_t))

            log_e = b_end + rcol
            m_loc = jnp.max(log_e, axis=0, keepdims=True)
            w_e = jnp.exp(log_e - m_loc)
            m_new = jnp.maximum(b_end + m_prev, m_loc)
            a = jnp.exp(b_end + m_prev - m_new)
            cc = jnp.exp(m_loc - m_new)
            kw = kc * w_e
            c_s[hd] = a * c_prev + cc * _dot_tn(kw.astype(BF16), vb)
            n_s[hd:hd + 1, :] = a * n_prev + cc * jnp.sum(kw, axis=0, keepdims=True)
            m_s[hd:hd + 1, :] = jnp.broadcast_to(m_new, (1, V7X_LANES))
        return carry

    lax.fori_loop(0, nchunk, chunk_body, 0)

    mo = proj(OFF_MO, OFF_ZG)
    ym = _dot((jax.nn.sigmoid(mo) * ht_s[...]).astype(BF16), wom_ref[...])

    def gate(i, y):
        zg = proj(OFF_ZG + i * D_MODEL, OFF_ZG + (i + 1) * D_MODEL) + gb_ref[i:i + 1, :]
        return jax.nn.sigmoid(zg) * y

    y = gate(0, ya) + gate(1, yc) + gate(2, ym)
    o_ref[0] = x + _dot(y.astype(BF16), wout_ref[...])

    if emit_state:
        @pl.when(t_idx == pl.num_programs(1) - 1)
        def _emit_state():
            k1, v1, u1, qk1, c1, n1, m1 = final_refs
            k1[...] = kext[0:BLOCK, :]
            v1[...] = vext_t[:, 0:BLOCK]
            u1[...] = uext[0:CONV_TAIL, :]
            qk1[...] = qkext[0:QK_TAIL, :]
            c1[...] = c_s[...]
            n1[...] = n_s[...]
            m1[...] = m_s[...]


def _mixer(hseq, p, consts, *, n_pad, pos0, init_state=None, emit_state=False):
    bsz, t_len, _ = hseq.shape
    tt = min(MIXER_TIME_TILE, t_len)
    assert t_len % tt == 0 and tt % BLOCK == 0 and tt % MLSTM_CHUNK == 0
    assert tt >= CONV_TAIL and tt >= BLOCK
    assert not emit_state or bsz == 1
    has_init = init_state is not None
    body = functools.partial(_mixer_body, tt=tt, n_pad=n_pad, pos0=pos0, has_init=has_init,
                             emit_state=emit_state)
    in_specs = [
        pl.BlockSpec((1, tt, D_MODEL), lambda b, t: (b, t, 0)),
        _const_spec((1, D_MODEL)),
        _const_spec((D_MODEL, N_IN_PACKED)),
        _const_spec((ATTN_WIDTH, D_MODEL)),
        _const_spec((KV_WIDTH, D_MODEL)),
        _const_spec((N_BRANCH, D_MODEL)),
        _const_spec((HEAD_DIM, BLOCK)),
        _const_spec((1, KV_WIDTH)),
        _const_spec((ATTN_KV_HEADS, ATTN_GROUP * BLOCK)),
        _const_spec((ATTN_WIDTH, D_MODEL)),
        _const_spec((CONV_KERNEL, CONV_WIDTH)),
        _const_spec((1, CONV_WIDTH)),
        _const_spec((1, CONV_WIDTH)),
        _const_spec((1, CONV_WIDTH)),
        _const_spec((CONV_WIDTH, D_MODEL)),
        _const_spec((QK_CONV_KERNEL, 2 * MLSTM_WIDTH)),
        _const_spec((1, 2 * MLSTM_WIDTH)),
        _const_spec((1, V7X_LANES)),
        _const_spec((MLSTM_WIDTH, D_MODEL)),
        _const_spec((D_MODEL, D_MODEL)),
        _const_spec((KV_WIDTH, KV_WIDTH)),
        _const_spec((2 * BLOCK, ATTN_GROUP * BLOCK)),
    ]
    args = [hseq, p["norm"], p["w_in"], p["w_q_t"], p["w_v_t"], p["gate_bias"], p["q_gain"],
            p["k_gain"], p["sinks"], p["w_o_attn"], p["conv_w"], p["conv_b"], p["ln_g"], p["ln_b"],
            p["w_o_conv"], p["qk_w"], p["qk_b"], p["if_bias"], p["w_o_mlstm"], p["w_out"],
            consts["ones_k"], consts["band"]]
    assert len(in_specs) == N_MIXER_PARAMS and len(args) == N_MIXER_PARAMS
    if has_init:
        in_specs += [_const_spec(shape) for shape, _ in STATE_SHAPES]
        args += list(init_state)
    out_specs = [pl.BlockSpec((1, tt, D_MODEL), lambda b, t: (b, t, 0))]
    out_shape = [jax.ShapeDtypeStruct(hseq.shape, F32)]
    if emit_state:
        for shape, dtype in STATE_SHAPES:
            out_specs.append(pl.BlockSpec(shape, lambda b, t, nd=len(shape): (0,) * nd))
            out_shape.append(jax.ShapeDtypeStruct(shape, dtype))
    scratch = [
        pltpu.VMEM((BLOCK + tt, KV_WIDTH), BF16),
        pltpu.VMEM((KV_WIDTH, BLOCK + tt), BF16),
        pltpu.VMEM((CONV_TAIL + tt, CONV_WIDTH), F32),
        pltpu.VMEM((V7X_SUBLANES - 1, tt + CONV_SHIFT_PAD, CONV_WIDTH), F32),
        pltpu.VMEM((QK_TAIL + tt, 2 * MLSTM_WIDTH), F32),
        pltpu.VMEM((tt, MLSTM_WIDTH), F32),
        pltpu.VMEM((tt, MLSTM_WIDTH), F32),
        pltpu.VMEM((tt, MLSTM_WIDTH), F32),
        pltpu.VMEM((tt, MLSTM_WIDTH), F32),
        pltpu.VMEM((tt, V7X_LANES), F32),
        pltpu.VMEM((tt, V7X_LANES), F32),
        pltpu.VMEM((tt, ATTN_WIDTH), BF16),
        pltpu.VMEM((MLSTM_HEADS, MLSTM_HEAD_DIM, MLSTM_HEAD_DIM), F32),
        pltpu.VMEM((V7X_SUBLANES, MLSTM_HEAD_DIM), F32),
        pltpu.VMEM((V7X_SUBLANES, V7X_LANES), F32),
    ]
    outs = pl.pallas_call(
        body,
        grid=(bsz, t_len // tt),
        in_specs=in_specs,
        out_specs=out_specs,
        out_shape=out_shape,
        scratch_shapes=scratch,
        compiler_params=pltpu.CompilerParams(
            dimension_semantics=("arbitrary", "arbitrary"),
            vmem_limit_bytes=MIXER_VMEM_LIMIT),
        name="mixer_lead" if emit_state else "mixer",
    )(*args)
    return (outs[0], tuple(outs[1:])) if emit_state else outs[0]


def _make_consts():
    i = jnp.arange(KV_WIDTH) // HEAD_DIM
    kj = jnp.arange(2 * BLOCK)
    qi = jnp.arange(ATTN_GROUP * BLOCK) % BLOCK
    allowed = (kj[:, None] > qi[None, :]) & (kj[:, None] <= qi[None, :] + BLOCK)
    return {
        "ones_k": (i[:, None] == i[None, :]).astype(BF16),
        "band": jnp.where(allowed, 0.0, NEG).astype(F32),
    }


def _pack_w_in(w_in):
    pad = jnp.zeros((D_MODEL, V7X_LANES - 2 * MLSTM_HEADS), w_in.dtype)
    return jnp.concatenate(
        [w_in[:, ORIG_OFF_AK:ORIG_OFF_AV], w_in[:, ORIG_OFF_CA:ORIG_OFF_IF], w_in[:, ORIG_OFF_ZG:],
         w_in[:, ORIG_OFF_IF:ORIG_OFF_ZG], pad], axis=1).astype(BF16)


def kernel(x, meta_tokens, ffn1_norm, ffn1_w1, ffn1_w3, ffn1_w2, mix_norm, w_in, gate_bias,
           attn_q_norm, attn_k_norm, attn_sinks, w_o_attn, conv_dw_w, conv_dw_b, conv_ln_g,
           conv_ln_b, w_o_conv, mlstm_qk_conv_w, mlstm_qk_conv_b, mlstm_igate_bias,
           mlstm_fgate_bias, w_o_mlstm, w_out, ffn2_norm, ffn2_w1, ffn2_w3, ffn2_w2):
    bsz, seq, _ = x.shape
    depth = w_in.shape[0]
    consts = _make_consts()

    def row(v):
        return v.reshape(1, -1).astype(F32)

    lead = jnp.concatenate(
        [jnp.zeros((N_PAD, D_MODEL), x.dtype), meta_tokens.astype(x.dtype)], axis=0)
    h = x

    for l in range(depth):
        if_bias = jnp.concatenate([
            mlstm_igate_bias[l], mlstm_fgate_bias[l],
            jnp.zeros((V7X_LANES - 2 * MLSTM_HEADS,), F32)]).reshape(1, V7X_LANES)
        mix = {
            "norm": row(mix_norm[l]),
            "w_in": _pack_w_in(w_in[l]),
            "w_q_t": w_in[l][:, ORIG_OFF_AQ:ORIG_OFF_AK].T.astype(BF16),
            "w_v_t": w_in[l][:, ORIG_OFF_AV:ORIG_OFF_CA].T.astype(BF16),
            "gate_bias": gate_bias[l].astype(F32),
            "q_gain": jnp.broadcast_to(attn_q_norm[l].astype(F32)[:, None], (HEAD_DIM, BLOCK)),
            "k_gain": row(jnp.tile(attn_k_norm[l], ATTN_KV_HEADS)),
            "sinks": jnp.repeat(attn_sinks[l].astype(F32), BLOCK).reshape(
                ATTN_KV_HEADS, ATTN_GROUP * BLOCK),
            "w_o_attn": w_o_attn[l].astype(BF16),
            "conv_w": conv_dw_w[l].astype(F32),
            "conv_b": row(conv_dw_b[l]),
            "ln_g": row(conv_ln_g[l]),
            "ln_b": row(conv_ln_b[l]),
            "w_o_conv": w_o_conv[l].astype(BF16),
            "qk_w": mlstm_qk_conv_w[l].astype(F32),
            "qk_b": row(mlstm_qk_conv_b[l]),
            "if_bias": if_bias,
            "w_o_mlstm": w_o_mlstm[l].astype(BF16),
            "w_out": w_out[l].astype(BF16),
        }
        ffn1 = (row(ffn1_norm[l]), ffn1_w1[l].astype(BF16), ffn1_w3[l].astype(BF16),
                ffn1_w2[l].astype(BF16))
        ffn2 = (row(ffn2_norm[l]), ffn2_w1[l].astype(BF16), ffn2_w3[l].astype(BF16),
                ffn2_w2[l].astype(BF16))

        lead = _ffn(lead, *ffn1)
        lead, state = _mixer(lead.reshape(1, LEAD, D_MODEL), mix, consts, n_pad=N_PAD, pos0=0,
                             emit_state=True)
        lead = lead.reshape(LEAD, D_MODEL)
        if l + 1 < depth:
            lead = _ffn(lead, *ffn2)

        h2 = _ffn(h.reshape(bsz * seq, D_MODEL), *ffn1)
        h = _mixer(h2.reshape(bsz, seq, D_MODEL), mix, consts, n_pad=N_PAD, pos0=LEAD,
                   init_state=state)
        h2 = _ffn(h.reshape(bsz * seq, D_MODEL), *ffn2)
        h = h2.reshape(bsz, seq, D_MODEL)
    return h
```

```python
import functools

import jax
import jax.numpy as jnp
from jax import lax
from jax.experimental import pallas as pl
from jax.experimental.pallas import tpu as pltpu

D_MODEL = 1024
N_META = 16
BLOCK = 128
LEAD = BLOCK
N_PAD = LEAD - N_META
ATTN_HEADS = 8
ATTN_KV_HEADS = 2
HEAD_DIM = 64
ATTN_GROUP = ATTN_HEADS // ATTN_KV_HEADS
ATTN_WIDTH = ATTN_HEADS * HEAD_DIM
KV_WIDTH = ATTN_KV_HEADS * HEAD_DIM
CONV_WIDTH = 512
CONV_KERNEL = 31
MLSTM_HEADS = 4
MLSTM_WIDTH = 512
MLSTM_HEAD_DIM = MLSTM_WIDTH // MLSTM_HEADS
MLSTM_CHUNK = 64
QK_CONV_KERNEL = 4
D_FF = 2816
N_BRANCH = 3
RMS_EPS = 1e-6
LN_EPS = 1e-5
NEG = -1e30

V7X_LANES = 128
V7X_SUBLANES = 8

ORIG_OFF_AQ = 0
ORIG_OFF_AK = ORIG_OFF_AQ + ATTN_WIDTH
ORIG_OFF_AV = ORIG_OFF_AK + KV_WIDTH
ORIG_OFF_CA = ORIG_OFF_AV + KV_WIDTH
ORIG_OFF_IF = ORIG_OFF_CA + 2 * CONV_WIDTH + 4 * MLSTM_WIDTH
ORIG_OFF_ZG = ORIG_OFF_IF + 2 * MLSTM_HEADS

OFF_AK = 0
OFF_CA = OFF_AK + KV_WIDTH
OFF_CG = OFF_CA + CONV_WIDTH
OFF_MQ = OFF_CG + CONV_WIDTH
OFF_MK = OFF_MQ + MLSTM_WIDTH
OFF_MV = OFF_MK + MLSTM_WIDTH
OFF_MO = OFF_MV + MLSTM_WIDTH
OFF_ZG = OFF_MO + MLSTM_WIDTH
OFF_IF = OFF_ZG + N_BRANCH * D_MODEL
N_IN_PACKED = OFF_IF + V7X_LANES

CONV_TAIL = 32
CONV_SHIFT_PAD = CONV_TAIL - V7X_SUBLANES
QK_TAIL = 8

FFN_ROW_TILE = 512
MIXER_TIME_TILE = 512
FFN_VMEM_LIMIT = 56 * 1024 * 1024
MIXER_VMEM_LIMIT = 60 * 1024 * 1024

F32 = jnp.float32
BF16 = jnp.bfloat16

STATE_SHAPES = (
    ((BLOCK, KV_WIDTH), BF16),
    ((KV_WIDTH, BLOCK), BF16),
    ((CONV_TAIL, CONV_WIDTH), F32),
    ((QK_TAIL, 2 * MLSTM_WIDTH), F32),
    ((MLSTM_HEADS, MLSTM_HEAD_DIM, MLSTM_HEAD_DIM), F32),
    ((V7X_SUBLANES, MLSTM_HEAD_DIM), F32),
    ((V7X_SUBLANES, V7X_LANES), F32),
)
N_MIXER_PARAMS = 22


def _const_spec(shape):
    nd = len(shape)
    return pl.BlockSpec(shape, lambda *_: (0,) * nd, pipeline_mode=pl.Buffered(1))


def _dot(a, b):
    return jnp.dot(a, b, preferred_element_type=F32)


def _dot_nt(a, b):
    return lax.dot_general(a, b, (((1,), (1,)), ((), ())), preferred_element_type=F32)


def _dot_tn(a, b):
    return lax.dot_general(a, b, (((0,), (0,)), ((), ())), preferred_element_type=F32)


def _rms_rows(x, gain):
    ms = jnp.mean(x * x, axis=-1, keepdims=True)
    return x * lax.rsqrt(ms + RMS_EPS) * gain


def _seg_sumsq(x, ones_ref):
    x2 = x * x
    hi = x2.astype(BF16)
    lo = (x2 - hi.astype(F32)).astype(BF16)
    ones = ones_ref[...]
    return _dot(hi, ones) + _dot(lo, ones)


def _ffn_body(x_ref, g_ref, w1_ref, w3_ref, w2_ref, o_ref):
    x = x_ref[...]
    h = _rms_rows(x, g_ref[...]).astype(BF16)
    a = _dot(h, w1_ref[...])
    b = _dot(h, w3_ref[...])
    u = (a * jax.nn.sigmoid(a) * b).astype(BF16)
    o_ref[...] = x + 0.5 * _dot(u, w2_ref[...])


def _ffn(x2d, g, w1, w3, w2):
    rows = x2d.shape[0]
    tm = min(FFN_ROW_TILE, rows)
    assert rows % tm == 0
    return pl.pallas_call(
        _ffn_body,
        grid=(rows // tm,),
        in_specs=[
            pl.BlockSpec((tm, D_MODEL), lambda i: (i, 0)),
            _const_spec((1, D_MODEL)),
            _const_spec((D_MODEL, D_FF)),
            _const_spec((D_MODEL, D_FF)),
            _const_spec((D_FF, D_MODEL)),
        ],
        out_specs=pl.BlockSpec((tm, D_MODEL), lambda i: (i, 0)),
        out_shape=jax.ShapeDtypeStruct(x2d.shape, F32),
        compiler_params=pltpu.CompilerParams(
            dimension_semantics=("arbitrary",), vmem_limit_bytes=FFN_VMEM_LIMIT),
        name="ffn",
    )(x2d, g, w1, w3, w2)


def _mixer_body(*refs, tt, n_pad, pos0, has_init, emit_state):
    n_state = len(STATE_SHAPES)
    n_in = N_MIXER_PARAMS + (n_state if has_init else 0)
    (x_ref, g_ref, win_ref, wqt_ref, wvt_ref, gb_ref, qg_ref, kg_ref, sink_ref,
     woa_ref, cw_ref, cb_ref, lng_ref, lnb_ref, woc_ref, qkw_ref, qkb_ref, ifb_ref,
     wom_ref, wout_ref, onesk_ref, band_ref) = refs[:N_MIXER_PARAMS]
    init_refs = refs[N_MIXER_PARAMS:n_in]
    o_ref = refs[n_in]
    n_out = 1 + (n_state if emit_state else 0)
    final_refs = refs[n_in + 1:n_in + n_out]
    (kext, vext_t, uext, ush, qkext, qm_s, km_s, vm_s, ht_s, bcum_s, r_s, ya_s,
     h_s, zg_s, c_s, n_s, m_s) = refs[n_in + n_out:]

    t_idx = pl.program_id(1)
    nblk = tt // BLOCK
    nchunk = tt // MLSTM_CHUNK
    mask_rows = pos0 < n_pad

    @pl.when(t_idx == 0)
    def _init_state():
        if has_init:
            k0, v0, u0, qk0, c0, n0, m0 = init_refs
            kext[0:BLOCK, :] = k0[...]
            vext_t[:, 0:BLOCK] = v0[...]
            uext[0:CONV_TAIL, :] = u0[...]
            qkext[0:QK_TAIL, :] = qk0[...]
            c_s[...] = c0[...]
            n_s[...] = n0[...]
            m_s[...] = m0[...]
        else:
            kext[0:BLOCK, :] = jnp.zeros((BLOCK, KV_WIDTH), BF16)
            vext_t[:, 0:BLOCK] = jnp.zeros((KV_WIDTH, BLOCK), BF16)
            uext[0:CONV_TAIL, :] = jnp.zeros((CONV_TAIL, CONV_WIDTH), F32)
            qkext[0:QK_TAIL, :] = jnp.zeros((QK_TAIL, 2 * MLSTM_WIDTH), F32)
            c_s[...] = jnp.zeros(c_s.shape, F32)
            n_s[...] = jnp.zeros(n_s.shape, F32)
            m_s[...] = jnp.full(m_s.shape, NEG, F32)

    t0 = t_idx * tt
    if mask_rows:
        valid_col = (lax.broadcasted_iota(jnp.int32, (tt, 1), 0) + (pos0 + t0)) >= n_pad

    def mask_pad_rows(v, fill):
        return jnp.where(valid_col, v, fill) if mask_rows else v

    x = x_ref[0]
    h = _rms_rows(x, g_ref[...]).astype(BF16)
    h_s[...] = h

    def proj(lo, hi):
        return _dot(h, win_ref[:, lo:hi])

    qt = _dot_nt(wqt_ref[...], h)
    vt = _dot_nt(wvt_ref[...], h)
    k = proj(OFF_AK, OFF_CA)
    k = k * lax.rsqrt(_seg_sumsq(k, onesk_ref) * (1.0 / HEAD_DIM) + RMS_EPS) * kg_ref[...]
    kext[BLOCK:BLOCK + tt, :] = k.astype(BF16)
    vext_t[:, BLOCK:BLOCK + tt] = vt.astype(BF16)

    qn = []
    for a in range(ATTN_HEADS):
        qa = qt[a * HEAD_DIM:(a + 1) * HEAD_DIM, :]
        inv = lax.rsqrt(jnp.mean(qa * qa, axis=0, keepdims=True) + RMS_EPS) * (HEAD_DIM ** -0.5)
        qn.append(qa * inv)
    qgain = qg_ref[...]
    band = band_ref[...]
    zeros_q = jnp.zeros((HEAD_DIM, ATTN_GROUP * BLOCK), BF16)
    key_iota = lax.broadcasted_iota(jnp.int32, (2 * BLOCK, ATTN_GROUP * BLOCK), 0)
    for n in range(nblk):
        r0 = n * BLOCK
        if pos0 + r0 - BLOCK >= n_pad:
            bias = band
        else:
            key_pos = key_iota + (pos0 + t0 + r0 - BLOCK)
            bias = jnp.where(key_pos >= n_pad, band, NEG)
        kcat = kext[r0:r0 + 2 * BLOCK, :]
        for hk in range(ATTN_KV_HEADS):
            qh = jnp.concatenate(
                [(qn[hk * ATTN_GROUP + g][:, r0:r0 + BLOCK] * qgain).astype(BF16)
                 for g in range(ATTN_GROUP)], axis=1)
            rhs = jnp.concatenate([qh, zeros_q] if hk == 0 else [zeros_q, qh], axis=0)
            st = _dot(kcat, rhs) + bias
            sink = sink_ref[hk:hk + 1, :]
            m = jnp.maximum(jnp.max(st, axis=0, keepdims=True), sink)
            p = jnp.exp(st - m)
            den = jnp.sum(p, axis=0, keepdims=True) + jnp.exp(sink - m)
            vth = vext_t[hk * HEAD_DIM:(hk + 1) * HEAD_DIM, r0:r0 + 2 * BLOCK]
            ot = _dot(vth, p.astype(BF16)) / den
            for j in range(ATTN_GROUP // 2):
                pair = jnp.concatenate(
                    [ot[:, (2 * j) * BLOCK:(2 * j + 1) * BLOCK],
                     ot[:, (2 * j + 1) * BLOCK:(2 * j + 2) * BLOCK]], axis=0)
                c0 = (hk * ATTN_GROUP + 2 * j) * HEAD_DIM
                ya_s[r0:r0 + BLOCK, c0:c0 + 2 * HEAD_DIM] = pair.T.astype(BF16)
    kext[0:BLOCK, :] = kext[tt:tt + BLOCK, :]
    vext_t[:, 0:BLOCK] = vext_t[:, tt:tt + BLOCK]
    ya = _dot(ya_s[...], woa_ref[...])

    ca = proj(OFF_CA, OFF_CG)
    cg = proj(OFF_CG, OFF_MQ)
    uext[CONV_TAIL:CONV_TAIL + tt, :] = mask_pad_rows(ca * jax.nn.sigmoid(cg), 0.0)
    for s in range(1, V7X_SUBLANES):
        ush[s - 1] = uext[s:s + tt + CONV_SHIFT_PAD, :]
    base = CONV_TAIL - (CONV_KERNEL - 1)
    acc = jnp.zeros((tt, CONV_WIDTH), F32) + cb_ref[...]
    for j in range(CONV_KERNEL):
        off = base + j
        s, a = off % V7X_SUBLANES, (off // V7X_SUBLANES) * V7X_SUBLANES
        tap = uext[a:a + tt, :] if s == 0 else ush[s - 1, a:a + tt, :]
        acc = acc + tap * cw_ref[j:j + 1, :]
    uext[0:CONV_TAIL, :] = uext[tt:tt + CONV_TAIL, :]
    mu = jnp.mean(acc, axis=-1, keepdims=True)
    xc = acc - mu
    yn = xc * lax.rsqrt(jnp.mean(xc * xc, axis=-1, keepdims=True) + LN_EPS)
    yn = yn * lng_ref[...] + lnb_ref[...]
    yc = _dot((yn * jax.nn.sigmoid(yn)).astype(BF16), woc_ref[...])

    mqk = proj(OFF_MQ, OFF_MV)
    qkext[QK_TAIL:QK_TAIL + tt, :] = mask_pad_rows(mqk, 0.0)
    qbase = QK_TAIL - (QK_CONV_KERNEL - 1)
    qacc = jnp.zeros((tt, 2 * MLSTM_WIDTH), F32) + qkb_ref[...]
    for j in range(QK_CONV_KERNEL):
        qacc = qacc + qkext[qbase + j:qbase + j + tt, :] * qkw_ref[j:j + 1, :]
    qkext[0:QK_TAIL, :] = qkext[tt:tt + QK_TAIL, :]
    qk = qacc * jax.nn.sigmoid(qacc)
    qm_s[...] = qk[:, :MLSTM_WIDTH] * (MLSTM_HEAD_DIM ** -0.5)
    km_s[...] = qk[:, MLSTM_WIDTH:]
    vm_s[...] = proj(OFF_MV, OFF_MO)

    zif = proj(OFF_IF, N_IN_PACKED) + ifb_ref[...]
    log_i = mask_pad_rows(zif, NEG)
    log_f = pltpu.roll(jax.nn.log_sigmoid(zif), V7X_LANES - MLSTM_HEADS, axis=1)
    pos = lax.broadcasted_iota(jnp.int32, (tt, 1), 0) % MLSTM_CHUNK
    bcum = log_f
    d = 1
    while d < MLSTM_CHUNK:
        bcum = bcum + jnp.where(pos >= d, pltpu.roll(bcum, d, axis=0), 0.0)
        d *= 2
    bcum_s[...] = bcum
    r_s[...] = log_i - bcum

    tri = (lax.broadcasted_iota(jnp.int32, (MLSTM_CHUNK, MLSTM_CHUNK), 0) >=
           lax.broadcasted_iota(jnp.int32, (MLSTM_CHUNK, MLSTM_CHUNK), 1))

    zg_slab = N_BRANCH * D_MODEL // nchunk
    assert zg_slab % V7X_LANES == 0

    def chunk_body(c, carry):
        z0 = pl.multiple_of(c * zg_slab, V7X_LANES)
        zg_s[:, pl.ds(z0, zg_slab)] = _dot(h_s[...], win_ref[:, pl.ds(OFF_ZG + z0, zg_slab)])

        r0 = pl.multiple_of(c * MLSTM_CHUNK, MLSTM_CHUNK)
        rows = pl.ds(r0, MLSTM_CHUNK)
        bc = bcum_s[rows, :]
        rc = r_s[rows, :]
        rct = rc.T
        for hd in range(MLSTM_HEADS):
            lanes = slice(hd * MLSTM_HEAD_DIM, (hd + 1) * MLSTM_HEAD_DIM)
            qc = qm_s[rows, lanes]
            kc = km_s[rows, lanes]
            vc = vm_s[rows, lanes]
            bcol = bc[:, hd:hd + 1]
            rcol = rc[:, hd:hd + 1]
            rrow = rct[hd:hd + 1, :]
            b_end = bc[MLSTM_CHUNK - 1:MLSTM_CHUNK, hd:hd + 1]
            m_prev = m_s[hd:hd + 1, 0:1]
            n_prev = n_s[hd:hd + 1, :]
            c_prev = c_s[hd]

            log_d = jnp.where(tri, bcol + rrow, NEG)
            m_inter = bcol + m_prev
            m_t = jnp.maximum(m_inter, jnp.max(log_d, axis=-1, keepdims=True))
            w_inter = jnp.exp(m_inter - m_t)
            qb = qc.astype(BF16)
            kb = kc.astype(BF16)
            vb = vc.astype(BF16)
            s = _dot_nt(qb, kb) * jnp.exp(log_d - m_t)
            num = _dot(s.astype(BF16), vb) + w_inter * _dot(qb, c_prev.astype(BF16))
            den = (jnp.sum(s, axis=-1, keepdims=True) +
                   w_inter * jnp.sum(qc * n_prev, axis=-1, keepdims=True))
            ht_s[rows, lanes] = num / jnp.maximum(jnp.abs(den), jnp.exp(-m_t))

            log_e = b_end + rcol
            m_loc = jnp.max(log_e, axis=0, keepdims=True)
            w_e = jnp.exp(log_e - m_loc)
            m_new = jnp.maximum(b_end + m_prev, m_loc)
            a = jnp.exp(b_end + m_prev - m_new)
            cc = jnp.exp(m_loc - m_new)
            kw = kc * w_e
            c_s[hd] = a * c_prev + cc * _dot_tn(kw.astype(BF16), vb)
            n_s[hd:hd + 1, :] = a * n_prev + cc * jnp.sum(kw, axis=0, keepdims=True)
            m_s[hd:hd + 1, :] = jnp.broadcast_to(m_new, (1, V7X_LANES))
        return carry

    lax.fori_loop(0, nchunk, chunk_body, 0)

    mo = proj(OFF_MO, OFF_ZG)
    ym = _dot((jax.nn.sigmoid(mo) * ht_s[...]).astype(BF16), wom_ref[...])

    def gate(i, y):
        zg = zg_s[:, i * D_MODEL:(i + 1) * D_MODEL] + gb_ref[i:i + 1, :]
        return jax.nn.sigmoid(zg) * y

    y = gate(0, ya) + gate(1, yc) + gate(2, ym)
    o_ref[0] = x + _dot(y.astype(BF16), wout_ref[...])

    if emit_state:
        @pl.when(t_idx == pl.num_programs(1) - 1)
        def _emit_state():
            k1, v1, u1, qk1, c1, n1, m1 = final_refs
            k1[...] = kext[0:BLOCK, :]
            v1[...] = vext_t[:, 0:BLOCK]
            u1[...] = uext[0:CONV_TAIL, :]
            qk1[...] = qkext[0:QK_TAIL, :]
            c1[...] = c_s[...]
            n1[...] = n_s[...]
            m1[...] = m_s[...]


def _mixer(hseq, p, consts, *, n_pad, pos0, init_state=None, emit_state=False):
    bsz, t_len, _ = hseq.shape
    tt = min(MIXER_TIME_TILE, t_len)
    assert t_len % tt == 0 and tt % BLOCK == 0 and tt % MLSTM_CHUNK == 0
    assert tt >= CONV_TAIL and tt >= BLOCK
    assert not emit_state or bsz == 1
    has_init = init_state is not None
    nchunk = tt // MLSTM_CHUNK
    body = functools.partial(_mixer_body, tt=tt, n_pad=n_pad, pos0=pos0, has_init=has_init,
                             emit_state=emit_state)
    in_specs = [
        pl.BlockSpec((1, tt, D_MODEL), lambda b, t: (b, t, 0)),
        _const_spec((1, D_MODEL)),
        _const_spec((D_MODEL, N_IN_PACKED)),
        _const_spec((ATTN_WIDTH, D_MODEL)),
        _const_spec((KV_WIDTH, D_MODEL)),
        _const_spec((N_BRANCH, D_MODEL)),
        _const_spec((HEAD_DIM, BLOCK)),
        _const_spec((1, KV_WIDTH)),
        _const_spec((ATTN_KV_HEADS, ATTN_GROUP * BLOCK)),
        _const_spec((ATTN_WIDTH, D_MODEL)),
        _const_spec((CONV_KERNEL, CONV_WIDTH)),
        _const_spec((1, CONV_WIDTH)),
        _const_spec((1, CONV_WIDTH)),
        _const_spec((1, CONV_WIDTH)),
        _const_spec((CONV_WIDTH, D_MODEL)),
        _const_spec((QK_CONV_KERNEL, 2 * MLSTM_WIDTH)),
        _const_spec((1, 2 * MLSTM_WIDTH)),
        _const_spec((1, V7X_LANES)),
        _const_spec((MLSTM_WIDTH, D_MODEL)),
        _const_spec((D_MODEL, D_MODEL)),
        _const_spec((KV_WIDTH, KV_WIDTH)),
        _const_spec((2 * BLOCK, ATTN_GROUP * BLOCK)),
    ]
    args = [hseq, p["norm"], p["w_in"], p["w_q_t"], p["w_v_t"], p["gate_bias"], p["q_gain"],
            p["k_gain"], p["sinks"], p["w_o_attn"], p["conv_w"], p["conv_b"], p["ln_g"], p["ln_b"],
            p["w_o_conv"], p["qk_w"], p["qk_b"], p["if_bias"], p["w_o_mlstm"], p["w_out"],
            consts["ones_k"], consts["band"]]
    assert len(in_specs) == N_MIXER_PARAMS and len(args) == N_MIXER_PARAMS
    if has_init:
        in_specs += [_const_spec(shape) for shape, _ in STATE_SHAPES]
        args += list(init_state)
    out_specs = [pl.BlockSpec((1, tt, D_MODEL), lambda b, t: (b, t, 0))]
    out_shape = [jax.ShapeDtypeStruct(hseq.shape, F32)]
    if emit_state:
        for shape, dtype in STATE_SHAPES:
            out_specs.append(pl.BlockSpec(shape, lambda b, t, nd=len(shape): (0,) * nd))
            out_shape.append(jax.ShapeDtypeStruct(shape, dtype))
    scratch = [
        pltpu.VMEM((BLOCK + tt, KV_WIDTH), BF16),
        pltpu.VMEM((KV_WIDTH, BLOCK + tt), BF16),
        pltpu.VMEM((CONV_TAIL + tt, CONV_WIDTH), F32),
        pltpu.VMEM((V7X_SUBLANES - 1, tt + CONV_SHIFT_PAD, CONV_WIDTH), F32),
        pltpu.VMEM((QK_TAIL + tt, 2 * MLSTM_WIDTH), F32),
        pltpu.VMEM((tt, MLSTM_WIDTH), F32),
        pltpu.VMEM((tt, MLSTM_WIDTH), F32),
        pltpu.VMEM((tt, MLSTM_WIDTH), F32),
        pltpu.VMEM((tt, MLSTM_WIDTH), F32),
        pltpu.VMEM((tt, V7X_LANES), F32),
        pltpu.VMEM((tt, V7X_LANES), F32),
        pltpu.VMEM((tt, ATTN_WIDTH), BF16),
        pltpu.VMEM((tt, D_MODEL), BF16),
        pltpu.VMEM((tt, N_BRANCH * D_MODEL), F32),
        pltpu.VMEM((MLSTM_HEADS, MLSTM_HEAD_DIM, MLSTM_HEAD_DIM), F32),
        pltpu.VMEM((V7X_SUBLANES, MLSTM_HEAD_DIM), F32),
        pltpu.VMEM((V7X_SUBLANES, V7X_LANES), F32),
    ]
    outs = pl.pallas_call(
        body,
        grid=(bsz, t_len // tt),
        in_specs=in_specs,
        out_specs=out_specs,
        out_shape=out_shape,
        scratch_shapes=scratch,
        compiler_params=pltpu.CompilerParams(
            dimension_semantics=("arbitrary", "arbitrary"),
            vmem_limit_bytes=MIXER_VMEM_LIMIT),
        name="mixer_lead" if emit_state else "mixer",
    )(*args)
    return (outs[0], tuple(outs[1:])) if emit_state else outs[0]


def _make_consts():
    i = jnp.arange(KV_WIDTH) // HEAD_DIM
    kj = jnp.arange(2 * BLOCK)
    qi = jnp.arange(ATTN_GROUP * BLOCK) % BLOCK
    allowed = (kj[:, None] > qi[None, :]) & (kj[:, None] <= qi[None, :] + BLOCK)
    return {
        "ones_k": (i[:, None] == i[None, :]).astype(BF16),
        "band": jnp.where(allowed, 0.0, NEG).astype(F32),
    }


def _pack_w_in(w_in):
    pad = jnp.zeros((D_MODEL, V7X_LANES - 2 * MLSTM_HEADS), w_in.dtype)
    return jnp.concatenate(
        [w_in[:, ORIG_OFF_AK:ORIG_OFF_AV], w_in[:, ORIG_OFF_CA:ORIG_OFF_IF], w_in[:, ORIG_OFF_ZG:],
         w_in[:, ORIG_OFF_IF:ORIG_OFF_ZG], pad], axis=1).astype(BF16)


def kernel(x, meta_tokens, ffn1_norm, ffn1_w1, ffn1_w3, ffn1_w2, mix_norm, w_in, gate_bias,
           attn_q_norm, attn_k_norm, attn_sinks, w_o_attn, conv_dw_w, conv_dw_b, conv_ln_g,
           conv_ln_b, w_o_conv, mlstm_qk_conv_w, mlstm_qk_conv_b, mlstm_igate_bias,
           mlstm_fgate_bias, w_o_mlstm, w_out, ffn2_norm, ffn2_w1, ffn2_w3, ffn2_w2):
    bsz, seq, _ = x.shape
    depth = w_in.shape[0]
    consts = _make_consts()

    def row(v):
        return v.reshape(1, -1).astype(F32)

    lead = jnp.concatenate(
        [jnp.zeros((N_PAD, D_MODEL), x.dtype), meta_tokens.astype(x.dtype)], axis=0)
    h = x

    for l in range(depth):
        if_bias = jnp.concatenate([
            mlstm_igate_bias[l], mlstm_fgate_bias[l],
            jnp.zeros((V7X_LANES - 2 * MLSTM_HEADS,), F32)]).reshape(1, V7X_LANES)
        mix = {
            "norm": row(mix_norm[l]),
            "w_in": _pack_w_in(w_in[l]),
            "w_q_t": w_in[l][:, ORIG_OFF_AQ:ORIG_OFF_AK].T.astype(BF16),
            "w_v_t": w_in[l][:, ORIG_OFF_AV:ORIG_OFF_CA].T.astype(BF16),
            "gate_bias": gate_bias[l].astype(F32),
            "q_gain": jnp.broadcast_to(attn_q_norm[l].astype(F32)[:, None], (HEAD_DIM, BLOCK)),
            "k_gain": row(jnp.tile(attn_k_norm[l], ATTN_KV_HEADS)),
            "sinks": jnp.repeat(attn_sinks[l].astype(F32), BLOCK).reshape(
                ATTN_KV_HEADS, ATTN_GROUP * BLOCK),
            "w_o_attn": w_o_attn[l].astype(BF16),
            "conv_w": conv_dw_w[l].astype(F32),
            "conv_b": row(conv_dw_b[l]),
            "ln_g": row(conv_ln_g[l]),
            "ln_b": row(conv_ln_b[l]),
            "w_o_conv": w_o_conv[l].astype(BF16),
            "qk_w": mlstm_qk_conv_w[l].astype(F32),
            "qk_b": row(mlstm_qk_conv_b[l]),
            "if_bias": if_bias,
            "w_o_mlstm": w_o_mlstm[l].astype(BF16),
            "w_out": w_out[l].astype(BF16),
        }
        ffn1 = (row(ffn1_norm[l]), ffn1_w1[l].astype(BF16), ffn1_w3[l].astype(BF16),
                ffn1_w2[l].astype(BF16))
        ffn2 = (row(ffn2_norm[l]), ffn2_w1[l].astype(BF16), ffn2_w3[l].astype(BF16),
                ffn2_w2[l].astype(BF16))

        lead = _ffn(lead, *ffn1)
        lead, state = _mixer(lead.reshape(1, LEAD, D_MODEL), mix, consts, n_pad=N_PAD, pos0=0,
                             emit_state=True)
        lead = lead.reshape(LEAD, D_MODEL)
        if l + 1 < depth:
            lead = _ffn(lead, *ffn2)

        h2 = _ffn(h.reshape(bsz * seq, D_MODEL), *ffn1)
        h = _mixer(h2.reshape(bsz, seq, D_MODEL), mix, consts, n_pad=N_PAD, pos0=LEAD,
                   init_state=state)
        h2 = _ffn(h.reshape(bsz * seq, D_MODEL), *ffn2)
        h = h2.reshape(bsz, seq, D_MODEL)
    return h
```

```python
import functools

import jax
import jax.numpy as jnp
from jax import lax
from jax.experimental import pallas as pl
from jax.experimental.pallas import tpu as pltpu

D_MODEL = 1024
N_META = 16
BLOCK = 128
LEAD = BLOCK
N_PAD = LEAD - N_META
ATTN_HEADS = 8
ATTN_KV_HEADS = 2
HEAD_DIM = 64
ATTN_GROUP = ATTN_HEADS // ATTN_KV_HEADS
ATTN_WIDTH = ATTN_HEADS * HEAD_DIM
KV_WIDTH = ATTN_KV_HEADS * HEAD_DIM
CONV_WIDTH = 512
CONV_KERNEL = 31
MLSTM_HEADS = 4
MLSTM_WIDTH = 512
MLSTM_HEAD_DIM = MLSTM_WIDTH // MLSTM_HEADS
MLSTM_CHUNK = 64
QK_CONV_KERNEL = 4
D_FF = 2816
N_BRANCH = 3
RMS_EPS = 1e-6
LN_EPS = 1e-5
NEG = -1e30

V7X_LANES = 128
V7X_SUBLANES = 8

ORIG_OFF_AQ = 0
ORIG_OFF_AK = ORIG_OFF_AQ + ATTN_WIDTH
ORIG_OFF_AV = ORIG_OFF_AK + KV_WIDTH
ORIG_OFF_CA = ORIG_OFF_AV + KV_WIDTH
ORIG_OFF_IF = ORIG_OFF_CA + 2 * CONV_WIDTH + 4 * MLSTM_WIDTH
ORIG_OFF_ZG = ORIG_OFF_IF + 2 * MLSTM_HEADS

OFF_AK = 0
OFF_CA = OFF_AK + KV_WIDTH
OFF_CG = OFF_CA + CONV_WIDTH
OFF_MQ = OFF_CG + CONV_WIDTH
OFF_MK = OFF_MQ + MLSTM_WIDTH
OFF_MV = OFF_MK + MLSTM_WIDTH
OFF_MO = OFF_MV + MLSTM_WIDTH
OFF_ZG = OFF_MO + MLSTM_WIDTH
OFF_IF = OFF_ZG + N_BRANCH * D_MODEL
N_IN_PACKED = OFF_IF + V7X_LANES

CONV_TAIL = 32
CONV_SHIFT_PAD = CONV_TAIL - V7X_SUBLANES
QK_TAIL = 8

FFN_ROW_TILE = 512
MIXER_TIME_TILE = 512
FFN_VMEM_LIMIT = 56 * 1024 * 1024
MIXER_VMEM_LIMIT = 63 * 1024 * 1024

F32 = jnp.float32
BF16 = jnp.bfloat16

STATE_SHAPES = (
    ((BLOCK, KV_WIDTH), BF16),
    ((KV_WIDTH, BLOCK), BF16),
    ((CONV_TAIL, CONV_WIDTH), F32),
    ((QK_TAIL, 2 * MLSTM_WIDTH), F32),
    ((MLSTM_HEADS, MLSTM_HEAD_DIM, MLSTM_HEAD_DIM), F32),
    ((V7X_SUBLANES, MLSTM_HEAD_DIM), F32),
    ((V7X_SUBLANES, V7X_LANES), F32),
)
N_MIXER_PARAMS = 22


def _const_spec(shape):
    nd = len(shape)
    return pl.BlockSpec(shape, lambda *_: (0,) * nd, pipeline_mode=pl.Buffered(1))


def _dot(a, b):
    return jnp.dot(a, b, preferred_element_type=F32)


def _dot_nt(a, b):
    return lax.dot_general(a, b, (((1,), (1,)), ((), ())), preferred_element_type=F32)


def _dot_tn(a, b):
    return lax.dot_general(a, b, (((0,), (0,)), ((), ())), preferred_element_type=F32)


def _rms_rows(x, gain):
    ms = jnp.mean(x * x, axis=-1, keepdims=True)
    return x * lax.rsqrt(ms + RMS_EPS) * gain


def _seg_sumsq(x, ones_ref):
    x2 = x * x
    hi = x2.astype(BF16)
    lo = (x2 - hi.astype(F32)).astype(BF16)
    ones = ones_ref[...]
    return _dot(hi, ones) + _dot(lo, ones)


def _ffn_body(x_ref, g_ref, w1_ref, w3_ref, w2_ref, o_ref):
    x = x_ref[...]
    h = _rms_rows(x, g_ref[...]).astype(BF16)
    a = _dot(h, w1_ref[...])
    b = _dot(h, w3_ref[...])
    u = (a * jax.nn.sigmoid(a) * b).astype(BF16)
    o_ref[...] = x + 0.5 * _dot(u, w2_ref[...])


def _ffn(x2d, g, w1, w3, w2):
    rows = x2d.shape[0]
    tm = min(FFN_ROW_TILE, rows)
    assert rows % tm == 0
    return pl.pallas_call(
        _ffn_body,
        grid=(rows // tm,),
        in_specs=[
            pl.BlockSpec((tm, D_MODEL), lambda i: (i, 0)),
            _const_spec((1, D_MODEL)),
            _const_spec((D_MODEL, D_FF)),
            _const_spec((D_MODEL, D_FF)),
            _const_spec((D_FF, D_MODEL)),
        ],
        out_specs=pl.BlockSpec((tm, D_MODEL), lambda i: (i, 0)),
        out_shape=jax.ShapeDtypeStruct(x2d.shape, F32),
        compiler_params=pltpu.CompilerParams(
            dimension_semantics=("arbitrary",), vmem_limit_bytes=FFN_VMEM_LIMIT),
        name="ffn",
    )(x2d, g, w1, w3, w2)


def _mixer_body(*refs, tt, n_pad, pos0, has_init, emit_state):
    n_state = len(STATE_SHAPES)
    n_in = N_MIXER_PARAMS + (n_state if has_init else 0)
    (x_ref, g_ref, win_ref, wqt_ref, wvt_ref, gb_ref, qg_ref, kg_ref, sink_ref,
     woa_ref, cw_ref, cb_ref, lng_ref, lnb_ref, woc_ref, qkw_ref, qkb_ref, ifb_ref,
     wom_ref, wout_ref, onesk_ref, band_ref) = refs[:N_MIXER_PARAMS]
    init_refs = refs[N_MIXER_PARAMS:n_in]
    o_ref = refs[n_in]
    n_out = 1 + (n_state if emit_state else 0)
    final_refs = refs[n_in + 1:n_in + n_out]
    (kext, vext_t, uext, work_s, qkext, qm_s, km_s, qn_s, ht_s, bcum_s, r_s, ya_s,
     h_s, pj_s, c_s, n_s, m_s) = refs[n_in + n_out:]

    t_idx = pl.program_id(1)
    nblk = tt // BLOCK
    nchunk = tt // MLSTM_CHUNK
    mask_rows = pos0 < n_pad

    @pl.when(t_idx == 0)
    def _init_state():
        if has_init:
            k0, v0, u0, qk0, c0, n0, m0 = init_refs
            kext[0:BLOCK, :] = k0[...]
            vext_t[:, 0:BLOCK] = v0[...]
            uext[0:CONV_TAIL, :] = u0[...]
            qkext[0:QK_TAIL, :] = qk0[...]
            c_s[...] = c0[...]
            n_s[...] = n0[...]
            m_s[...] = m0[...]
        else:
            kext[0:BLOCK, :] = jnp.zeros((BLOCK, KV_WIDTH), BF16)
            vext_t[:, 0:BLOCK] = jnp.zeros((KV_WIDTH, BLOCK), BF16)
            uext[0:CONV_TAIL, :] = jnp.zeros((CONV_TAIL, CONV_WIDTH), F32)
            qkext[0:QK_TAIL, :] = jnp.zeros((QK_TAIL, 2 * MLSTM_WIDTH), F32)
            c_s[...] = jnp.zeros(c_s.shape, F32)
            n_s[...] = jnp.zeros(n_s.shape, F32)
            m_s[...] = jnp.full(m_s.shape, NEG, F32)

    t0 = t_idx * tt
    if mask_rows:
        valid_col = (lax.broadcasted_iota(jnp.int32, (tt, 1), 0) + (pos0 + t0)) >= n_pad

    def mask_pad_rows(v, fill):
        return jnp.where(valid_col, v, fill) if mask_rows else v

    x = x_ref[0]
    h = _rms_rows(x, g_ref[...]).astype(BF16)
    h_s[...] = h

    def proj(lo, hi):
        return _dot(h, win_ref[:, lo:hi])

    qt = _dot_nt(wqt_ref[...], h)
    vt = _dot_nt(wvt_ref[...], h)
    k = proj(OFF_AK, OFF_CA)
    k = k * lax.rsqrt(_seg_sumsq(k, onesk_ref) * (1.0 / HEAD_DIM) + RMS_EPS) * kg_ref[...]
    kext[BLOCK:BLOCK + tt, :] = k.astype(BF16)
    vext_t[:, BLOCK:BLOCK + tt] = vt.astype(BF16)

    for a in range(ATTN_HEADS):
        qa = qt[a * HEAD_DIM:(a + 1) * HEAD_DIM, :]
        inv = lax.rsqrt(jnp.mean(qa * qa, axis=0, keepdims=True) + RMS_EPS) * (HEAD_DIM ** -0.5)
        qn_s[a * HEAD_DIM:(a + 1) * HEAD_DIM, :] = qa * inv
    qgain = qg_ref[...]
    band = band_ref[...]
    zeros_q = jnp.zeros((HEAD_DIM, ATTN_GROUP * BLOCK), BF16)
    key_iota = lax.broadcasted_iota(jnp.int32, (2 * BLOCK, ATTN_GROUP * BLOCK), 0)
    pj_slab = (OFF_ZG - OFF_CA) // nblk
    pj_piece = pj_slab // ATTN_KV_HEADS
    assert pj_piece % V7X_LANES == 0

    def attn_body(n, carry):
        def project_piece(i):
            z0 = pl.multiple_of(n * pj_slab + i * pj_piece, V7X_LANES)
            pj_s[:, pl.ds(z0, pj_piece)] = _dot(h_s[...], win_ref[:, pl.ds(OFF_CA + z0, pj_piece)])

        r0 = pl.multiple_of(n * BLOCK, BLOCK)
        key_pos = key_iota + (pos0 + t0 + r0 - BLOCK)
        bias = jnp.where(key_pos >= n_pad, band, NEG)
        kcat = kext[pl.ds(r0, 2 * BLOCK), :]
        for hk in range(ATTN_KV_HEADS):
            qh = jnp.concatenate(
                [(qn_s[(hk * ATTN_GROUP + g) * HEAD_DIM:(hk * ATTN_GROUP + g + 1) * HEAD_DIM,
                       pl.ds(r0, BLOCK)] * qgain).astype(BF16)
                 for g in range(ATTN_GROUP)], axis=1)
            rhs = jnp.concatenate([qh, zeros_q] if hk == 0 else [zeros_q, qh], axis=0)
            st = _dot(kcat, rhs) + bias
            project_piece(hk)
            sink = sink_ref[hk:hk + 1, :]
            m = jnp.maximum(jnp.max(st, axis=0, keepdims=True), sink)
            p = jnp.exp(st - m)
            den = jnp.sum(p, axis=0, keepdims=True) + jnp.exp(sink - m)
            vth = vext_t[hk * HEAD_DIM:(hk + 1) * HEAD_DIM, pl.ds(r0, 2 * BLOCK)]
            ot = _dot(vth, p.astype(BF16)) / den
            for j in range(ATTN_GROUP // 2):
                pair = jnp.concatenate(
                    [ot[:, (2 * j) * BLOCK:(2 * j + 1) * BLOCK],
                     ot[:, (2 * j + 1) * BLOCK:(2 * j + 2) * BLOCK]], axis=0)
                c0 = (hk * ATTN_GROUP + 2 * j) * HEAD_DIM
                ya_s[pl.ds(r0, BLOCK), c0:c0 + 2 * HEAD_DIM] = pair.T.astype(BF16)
        return carry

    lax.fori_loop(0, nblk, attn_body, 0)
    kext[0:BLOCK, :] = kext[tt:tt + BLOCK, :]
    vext_t[:, 0:BLOCK] = vext_t[:, tt:tt + BLOCK]
    ya = _dot(ya_s[...], woa_ref[...])

    def projected(lo, hi):
        return pj_s[:, lo - OFF_CA:hi - OFF_CA]

    ca = projected(OFF_CA, OFF_CG)
    cg = projected(OFF_CG, OFF_MQ)
    uext[CONV_TAIL:CONV_TAIL + tt, :] = mask_pad_rows(ca * jax.nn.sigmoid(cg), 0.0)
    sh_rows = tt + CONV_SHIFT_PAD
    for s in range(1, V7X_SUBLANES):
        work_s[(s - 1) * sh_rows:s * sh_rows, :] = uext[s:s + sh_rows, :]
    base = CONV_TAIL - (CONV_KERNEL - 1)
    acc = jnp.zeros((tt, CONV_WIDTH), F32) + cb_ref[...]
    for j in range(CONV_KERNEL):
        off = base + j
        s, a = off % V7X_SUBLANES, (off // V7X_SUBLANES) * V7X_SUBLANES
        tap = uext[a:a + tt, :] if s == 0 else work_s[(s - 1) * sh_rows + a:(s - 1) * sh_rows + a + tt, :]
        acc = acc + tap * cw_ref[j:j + 1, :]
    uext[0:CONV_TAIL, :] = uext[tt:tt + CONV_TAIL, :]
    mu = jnp.mean(acc, axis=-1, keepdims=True)
    xc = acc - mu
    yn = xc * lax.rsqrt(jnp.mean(xc * xc, axis=-1, keepdims=True) + LN_EPS)
    yn = yn * lng_ref[...] + lnb_ref[...]
    yc = _dot((yn * jax.nn.sigmoid(yn)).astype(BF16), woc_ref[...])

    mqk = projected(OFF_MQ, OFF_MV)
    qkext[QK_TAIL:QK_TAIL + tt, :] = mask_pad_rows(mqk, 0.0)
    qbase = QK_TAIL - (QK_CONV_KERNEL - 1)
    qacc = jnp.zeros((tt, 2 * MLSTM_WIDTH), F32) + qkb_ref[...]
    for j in range(QK_CONV_KERNEL):
        qacc = qacc + qkext[qbase + j:qbase + j + tt, :] * qkw_ref[j:j + 1, :]
    qkext[0:QK_TAIL, :] = qkext[tt:tt + QK_TAIL, :]
    qk = qacc * jax.nn.sigmoid(qacc)
    qm_s[...] = qk[:, :MLSTM_WIDTH] * (MLSTM_HEAD_DIM ** -0.5)
    km_s[...] = qk[:, MLSTM_WIDTH:]

    zif = proj(OFF_IF, N_IN_PACKED) + ifb_ref[...]
    log_i = mask_pad_rows(zif, NEG)
    log_f = pltpu.roll(jax.nn.log_sigmoid(zif), V7X_LANES - MLSTM_HEADS, axis=1)
    pos = lax.broadcasted_iota(jnp.int32, (tt, 1), 0) % MLSTM_CHUNK
    bcum = log_f
    d = 1
    while d < MLSTM_CHUNK:
        bcum = bcum + jnp.where(pos >= d, pltpu.roll(bcum, d, axis=0), 0.0)
        d *= 2
    bcum_s[...] = bcum
    r_s[...] = log_i - bcum

    tri = (lax.broadcasted_iota(jnp.int32, (MLSTM_CHUNK, MLSTM_CHUNK), 0) >=
           lax.broadcasted_iota(jnp.int32, (MLSTM_CHUNK, MLSTM_CHUNK), 1))

    zg_slab = N_BRANCH * D_MODEL // nchunk
    lanes_per_blk = CONV_WIDTH // V7X_LANES
    zg_blocks = N_BRANCH * D_MODEL // CONV_WIDTH
    assert zg_slab % V7X_LANES == 0 and zg_blocks * tt <= work_s.shape[0]

    def chunk_body(c, carry):
        def gate_project():
            z0 = pl.multiple_of(c * zg_slab, V7X_LANES)
            res = _dot(h_s[...], win_ref[:, pl.ds(OFF_ZG + z0, zg_slab)])
            for piece in range(zg_slab // V7X_LANES):
                q = c * (zg_slab // V7X_LANES) + piece
                blk_row = pl.multiple_of((q // lanes_per_blk) * tt, V7X_SUBLANES)
                lane0 = pl.multiple_of((q % lanes_per_blk) * V7X_LANES, V7X_LANES)
                work_s[pl.ds(blk_row, tt), pl.ds(lane0, V7X_LANES)] = (
                    res[:, piece * V7X_LANES:(piece + 1) * V7X_LANES])

        r0 = pl.multiple_of(c * MLSTM_CHUNK, MLSTM_CHUNK)
        rows = pl.ds(r0, MLSTM_CHUNK)
        bc = bcum_s[rows, :]
        rc = r_s[rows, :]
        rct = rc.T
        for hd in range(MLSTM_HEADS):
            lanes = slice(hd * MLSTM_HEAD_DIM, (hd + 1) * MLSTM_HEAD_DIM)
            qc = qm_s[rows, lanes]
            kc = km_s[rows, lanes]
            vc = pj_s[rows, OFF_MV - OFF_CA + hd * MLSTM_HEAD_DIM:OFF_MV - OFF_CA + (hd + 1) * MLSTM_HEAD_DIM]
            bcol = bc[:, hd:hd + 1]
            rcol = rc[:, hd:hd + 1]
            rrow = rct[hd:hd + 1, :]
            b_end = bc[MLSTM_CHUNK - 1:MLSTM_CHUNK, hd:hd + 1]
            m_prev = m_s[hd:hd + 1, 0:1]
            n_prev = n_s[hd:hd + 1, :]
            c_prev = c_s[hd]

            log_d = jnp.where(tri, bcol + rrow, NEG)
            m_inter = bcol + m_prev
            m_t = jnp.maximum(m_inter, jnp.max(log_d, axis=-1, keepdims=True))
            w_inter = jnp.exp(m_inter - m_t)
            qb = qc.astype(BF16)
            kb = kc.astype(BF16)
            vb = vc.astype(BF16)
            scores = _dot_nt(qb, kb)
            carried = _dot(qb, c_prev.astype(BF16))
            if hd == 0:
                gate_project()
            s = scores * jnp.exp(log_d - m_t)
            num = _dot(s.astype(BF16), vb) + w_inter * carried
            den = (jnp.sum(s, axis=-1, keepdims=True) +
                   w_inter * jnp.sum(qc * n_prev, axis=-1, keepdims=True))
            ht_s[rows, lanes] = num / jnp.maximum(jnp.abs(den), jnp.exp(-m_t))

            log_e = b_end + rcol
            m_loc = jnp.max(log_e, axis=0, keepdims=True)
            w_e = jnp.exp(log_e - m_loc)
            m_new = jnp.maximum(b_end + m_prev, m_loc)
            a = jnp.exp(b_end + m_prev - m_new)
            cc = jnp.exp(m_loc - m_new)
            kw = kc * w_e
            c_s[hd] = a * c_prev + cc * _dot_tn(kw.astype(BF16), vb)
            n_s[hd:hd + 1, :] = a * n_prev + cc * jnp.sum(kw, axis=0, keepdims=True)
            m_s[hd:hd + 1, :] = jnp.broadcast_to(m_new, (1, V7X_LANES))
        return carry

    lax.fori_loop(0, nchunk, chunk_body, 0)

    mo = projected(OFF_MO, OFF_ZG)
    ym = _dot((jax.nn.sigmoid(mo) * ht_s[...]).astype(BF16), wom_ref[...])

    def gate(i, y):
        blks = range(i * D_MODEL // CONV_WIDTH, (i + 1) * D_MODEL // CONV_WIDTH)
        zg = jnp.concatenate([work_s[b * tt:(b + 1) * tt, :] for b in blks], axis=1) + gb_ref[i:i + 1, :]
        return jax.nn.sigmoid(zg) * y

    y = gate(0, ya) + gate(1, yc) + gate(2, ym)
    o_ref[0] = x + _dot(y.astype(BF16), wout_ref[...])

    if emit_state:
        @pl.when(t_idx == pl.num_programs(1) - 1)
        def _emit_state():
            k1, v1, u1, qk1, c1, n1, m1 = final_refs
            k1[...] = kext[0:BLOCK, :]
            v1[...] = vext_t[:, 0:BLOCK]
            u1[...] = uext[0:CONV_TAIL, :]
            qk1[...] = qkext[0:QK_TAIL, :]
            c1[...] = c_s[...]
            n1[...] = n_s[...]
            m1[...] = m_s[...]


def _mixer(hseq, p, consts, *, n_pad, pos0, init_state=None, emit_state=False):
    bsz, t_len, _ = hseq.shape
    tt = min(MIXER_TIME_TILE, t_len)
    assert t_len % tt == 0 and tt % BLOCK == 0 and tt % MLSTM_CHUNK == 0
    assert tt >= CONV_TAIL and tt >= BLOCK
    assert not emit_state or bsz == 1
    has_init = init_state is not None
    nchunk = tt // MLSTM_CHUNK
    body = functools.partial(_mixer_body, tt=tt, n_pad=n_pad, pos0=pos0, has_init=has_init,
                             emit_state=emit_state)
    in_specs = [
        pl.BlockSpec((1, tt, D_MODEL), lambda b, t: (b, t, 0)),
        _const_spec((1, D_MODEL)),
        _const_spec((D_MODEL, N_IN_PACKED)),
        _const_spec((ATTN_WIDTH, D_MODEL)),
        _const_spec((KV_WIDTH, D_MODEL)),
        _const_spec((N_BRANCH, D_MODEL)),
        _const_spec((HEAD_DIM, BLOCK)),
        _const_spec((1, KV_WIDTH)),
        _const_spec((ATTN_KV_HEADS, ATTN_GROUP * BLOCK)),
        _const_spec((ATTN_WIDTH, D_MODEL)),
        _const_spec((CONV_KERNEL, CONV_WIDTH)),
        _const_spec((1, CONV_WIDTH)),
        _const_spec((1, CONV_WIDTH)),
        _const_spec((1, CONV_WIDTH)),
        _const_spec((CONV_WIDTH, D_MODEL)),
        _const_spec((QK_CONV_KERNEL, 2 * MLSTM_WIDTH)),
        _const_spec((1, 2 * MLSTM_WIDTH)),
        _const_spec((1, V7X_LANES)),
        _const_spec((MLSTM_WIDTH, D_MODEL)),
        _const_spec((D_MODEL, D_MODEL)),
        _const_spec((KV_WIDTH, KV_WIDTH)),
        _const_spec((2 * BLOCK, ATTN_GROUP * BLOCK)),
    ]
    args = [hseq, p["norm"], p["w_in"], p["w_q_t"], p["w_v_t"], p["gate_bias"], p["q_gain"],
            p["k_gain"], p["sinks"], p["w_o_attn"], p["conv_w"], p["conv_b"], p["ln_g"], p["ln_b"],
            p["w_o_conv"], p["qk_w"], p["qk_b"], p["if_bias"], p["w_o_mlstm"], p["w_out"],
            consts["ones_k"], consts["band"]]
    assert len(in_specs) == N_MIXER_PARAMS and len(args) == N_MIXER_PARAMS
    if has_init:
        in_specs += [_const_spec(shape) for shape, _ in STATE_SHAPES]
        args += list(init_state)
    out_specs = [pl.BlockSpec((1, tt, D_MODEL), lambda b, t: (b, t, 0))]
    out_shape = [jax.ShapeDtypeStruct(hseq.shape, F32)]
    if emit_state:
        for shape, dtype in STATE_SHAPES:
            out_specs.append(pl.BlockSpec(shape, lambda b, t, nd=len(shape): (0,) * nd))
            out_shape.append(jax.ShapeDtypeStruct(shape, dtype))
    scratch = [
        pltpu.VMEM((BLOCK + tt, KV_WIDTH), BF16),
        pltpu.VMEM((KV_WIDTH, BLOCK + tt), BF16),
        pltpu.VMEM((CONV_TAIL + tt, CONV_WIDTH), F32),
        pltpu.VMEM(((V7X_SUBLANES - 1) * (tt + CONV_SHIFT_PAD), CONV_WIDTH), F32),
        pltpu.VMEM((QK_TAIL + tt, 2 * MLSTM_WIDTH), F32),
        pltpu.VMEM((tt, MLSTM_WIDTH), F32),
        pltpu.VMEM((tt, MLSTM_WIDTH), F32),
        pltpu.VMEM((ATTN_WIDTH, tt), F32),
        pltpu.VMEM((tt, MLSTM_WIDTH), F32),
        pltpu.VMEM((tt, V7X_LANES), F32),
        pltpu.VMEM((tt, V7X_LANES), F32),
        pltpu.VMEM((tt, ATTN_WIDTH), BF16),
        pltpu.VMEM((tt, D_MODEL), BF16),
        pltpu.VMEM((tt, OFF_ZG - OFF_CA), F32),
        pltpu.VMEM((MLSTM_HEADS, MLSTM_HEAD_DIM, MLSTM_HEAD_DIM), F32),
        pltpu.VMEM((V7X_SUBLANES, MLSTM_HEAD_DIM), F32),
        pltpu.VMEM((V7X_SUBLANES, V7X_LANES), F32),
    ]
    outs = pl.pallas_call(
        body,
        grid=(bsz, t_len // tt),
        in_specs=in_specs,
        out_specs=out_specs,
        out_shape=out_shape,
        scratch_shapes=scratch,
        compiler_params=pltpu.CompilerParams(
            dimension_semantics=("arbitrary", "arbitrary"),
            vmem_limit_bytes=MIXER_VMEM_LIMIT),
        name="mixer_lead" if emit_state else "mixer",
    )(*args)
    return (outs[0], tuple(outs[1:])) if emit_state else outs[0]


def _make_consts():
    i = jnp.arange(KV_WIDTH) // HEAD_DIM
    kj = jnp.arange(2 * BLOCK)
    qi = jnp.arange(ATTN_GROUP * BLOCK) % BLOCK
    allowed = (kj[:, None] > qi[None, :]) & (kj[:, None] <= qi[None, :] + BLOCK)
    return {
        "ones_k": (i[:, None] == i[None, :]).astype(BF16),
        "band": jnp.where(allowed, 0.0, NEG).astype(F32),
    }


def _pack_w_in(w_in):
    pad = jnp.zeros((D_MODEL, V7X_LANES - 2 * MLSTM_HEADS), w_in.dtype)
    return jnp.concatenate(
        [w_in[:, ORIG_OFF_AK:ORIG_OFF_AV], w_in[:, ORIG_OFF_CA:ORIG_OFF_IF], w_in[:, ORIG_OFF_ZG:],
         w_in[:, ORIG_OFF_IF:ORIG_OFF_ZG], pad], axis=1).astype(BF16)


def kernel(x, meta_tokens, ffn1_norm, ffn1_w1, ffn1_w3, ffn1_w2, mix_norm, w_in, gate_bias,
           attn_q_norm, attn_k_norm, attn_sinks, w_o_attn, conv_dw_w, conv_dw_b, conv_ln_g,
           conv_ln_b, w_o_conv, mlstm_qk_conv_w, mlstm_qk_conv_b, mlstm_igate_bias,
           mlstm_fgate_bias, w_o_mlstm, w_out, ffn2_norm, ffn2_w1, ffn2_w3, ffn2_w2):
    bsz, seq, _ = x.shape
    depth = w_in.shape[0]
    consts = _make_consts()

    def row(v):
        return v.reshape(1, -1).astype(F32)

    lead = jnp.concatenate(
        [jnp.zeros((N_PAD, D_MODEL), x.dtype), meta_tokens.astype(x.dtype)], axis=0)
    h = x

    for l in range(depth):
        if_bias = jnp.concatenate([
            mlstm_igate_bias[l], mlstm_fgate_bias[l],
            jnp.zeros((V7X_LANES - 2 * MLSTM_HEADS,), F32)]).reshape(1, V7X_LANES)
        mix = {
            "norm": row(mix_norm[l]),
            "w_in": _pack_w_in(w_in[l]),
            "w_q_t": w_in[l][:, ORIG_OFF_AQ:ORIG_OFF_AK].T.astype(BF16),
            "w_v_t": w_in[l][:, ORIG_OFF_AV:ORIG_OFF_CA].T.astype(BF16),
            "gate_bias": gate_bias[l].astype(F32),
            "q_gain": jnp.broadcast_to(attn_q_norm[l].astype(F32)[:, None], (HEAD_DIM, BLOCK)),
            "k_gain": row(jnp.tile(attn_k_norm[l], ATTN_KV_HEADS)),
            "sinks": jnp.repeat(attn_sinks[l].astype(F32), BLOCK).reshape(
                ATTN_KV_HEADS, ATTN_GROUP * BLOCK),
            "w_o_attn": w_o_attn[l].astype(BF16),
            "conv_w": conv_dw_w[l].astype(F32),
            "conv_b": row(conv_dw_b[l]),
            "ln_g": row(conv_ln_g[l]),
            "ln_b": row(conv_ln_b[l]),
            "w_o_conv": w_o_conv[l].astype(BF16),
            "qk_w": mlstm_qk_conv_w[l].astype(F32),
            "qk_b": row(mlstm_qk_conv_b[l]),
            "if_bias": if_bias,
            "w_o_mlstm": w_o_mlstm[l].astype(BF16),
            "w_out": w_out[l].astype(BF16),
        }
        ffn1 = (row(ffn1_norm[l]), ffn1_w1[l].astype(BF16), ffn1_w3[l].astype(BF16),
                ffn1_w2[l].astype(BF16))
        ffn2 = (row(ffn2_norm[l]), ffn2_w1[l].astype(BF16), ffn2_w3[l].astype(BF16),
                ffn2_w2[l].astype(BF16))

        lead = _ffn(lead, *ffn1)
        lead, state = _mixer(lead.reshape(1, LEAD, D_MODEL), mix, consts, n_pad=N_PAD, pos0=0,
                             emit_state=True)
        lead = lead.reshape(LEAD, D_MODEL)
        if l + 1 < depth:
            lead = _ffn(lead, *ffn2)

        h2 = _ffn(h.reshape(bsz * seq, D_MODEL), *ffn1)
        h = _mixer(h2.reshape(bsz, seq, D_MODEL), mix, consts, n_pad=N_PAD, pos0=LEAD,
                   init_state=state)
        h2 = _ffn(h.reshape(bsz * seq, D_MODEL), *ffn2)
        h = h2.reshape(bsz, seq, D_MODEL)
    return h
```

```python
import functools

import jax
import jax.numpy as jnp
from jax import lax
from jax.experimental import pallas as pl
from jax.experimental.pallas import tpu as pltpu

D_MODEL = 1024
N_META = 16
BLOCK = 128
LEAD = BLOCK
N_PAD = LEAD - N_META
ATTN_HEADS = 8
ATTN_KV_HEADS = 2
HEAD_DIM = 64
ATTN_GROUP = ATTN_HEADS // ATTN_KV_HEADS
ATTN_WIDTH = ATTN_HEADS * HEAD_DIM
KV_WIDTH = ATTN_KV_HEADS * HEAD_DIM
CONV_WIDTH = 512
CONV_KERNEL = 31
MLSTM_HEADS = 4
MLSTM_WIDTH = 512
MLSTM_HEAD_DIM = MLSTM_WIDTH // MLSTM_HEADS
MLSTM_CHUNK = 64
QK_CONV_KERNEL = 4
D_FF = 2816
N_BRANCH = 3
RMS_EPS = 1e-6
LN_EPS = 1e-5
NEG = -1e30

V7X_LANES = 128
V7X_SUBLANES = 8

ORIG_OFF_AQ = 0
ORIG_OFF_AK = ORIG_OFF_AQ + ATTN_WIDTH
ORIG_OFF_AV = ORIG_OFF_AK + KV_WIDTH
ORIG_OFF_CA = ORIG_OFF_AV + KV_WIDTH
ORIG_OFF_IF = ORIG_OFF_CA + 2 * CONV_WIDTH + 4 * MLSTM_WIDTH
ORIG_OFF_ZG = ORIG_OFF_IF + 2 * MLSTM_HEADS

OFF_AK = 0
OFF_CA = OFF_AK + KV_WIDTH
OFF_CG = OFF_CA + CONV_WIDTH
OFF_MQ = OFF_CG + CONV_WIDTH
OFF_MK = OFF_MQ + MLSTM_WIDTH
OFF_MV = OFF_MK + MLSTM_WIDTH
OFF_MO = OFF_MV + MLSTM_WIDTH
OFF_ZG = OFF_MO + MLSTM_WIDTH
OFF_IF = OFF_ZG + N_BRANCH * D_MODEL
N_IN_PACKED = OFF_IF + V7X_LANES

CONV_TAIL = 32
CONV_SHIFT_PAD = CONV_TAIL - V7X_SUBLANES
QK_TAIL = 8

FFN_ROW_TILE = 512
MIXER_TIME_TILE = 512
FFN_VMEM_LIMIT = 56 * 1024 * 1024
MIXER_VMEM_LIMIT = 60 * 1024 * 1024

F32 = jnp.float32
BF16 = jnp.bfloat16

STATE_SHAPES = (
    ((BLOCK, KV_WIDTH), BF16),
    ((KV_WIDTH, BLOCK), BF16),
    ((CONV_TAIL, CONV_WIDTH), F32),
    ((QK_TAIL, 2 * MLSTM_WIDTH), F32),
    ((MLSTM_HEADS, MLSTM_HEAD_DIM, MLSTM_HEAD_DIM), F32),
    ((V7X_SUBLANES, MLSTM_HEAD_DIM), F32),
    ((V7X_SUBLANES, V7X_LANES), F32),
)
N_MIXER_PARAMS = 22


def _const_spec(shape):
    nd = len(shape)
    return pl.BlockSpec(shape, lambda *_: (0,) * nd, pipeline_mode=pl.Buffered(1))


def _dot(a, b):
    return jnp.dot(a, b, preferred_element_type=F32)


def _dot_nt(a, b):
    return lax.dot_general(a, b, (((1,), (1,)), ((), ())), preferred_element_type=F32)


def _dot_tn(a, b):
    return lax.dot_general(a, b, (((0,), (0,)), ((), ())), preferred_element_type=F32)


def _rms_rows(x, gain):
    ms = jnp.mean(x * x, axis=-1, keepdims=True)
    return x * lax.rsqrt(ms + RMS_EPS) * gain


def _seg_sumsq(x, ones_ref):
    x2 = x * x
    hi = x2.astype(BF16)
    lo = (x2 - hi.astype(F32)).astype(BF16)
    ones = ones_ref[...]
    return _dot(hi, ones) + _dot(lo, ones)


def _ffn_body(x_ref, g_ref, w1_ref, w3_ref, w2_ref, o_ref):
    x = x_ref[...]
    h = _rms_rows(x, g_ref[...]).astype(BF16)
    a = _dot(h, w1_ref[...])
    b = _dot(h, w3_ref[...])
    u = (a * jax.nn.sigmoid(a) * b).astype(BF16)
    o_ref[...] = x + 0.5 * _dot(u, w2_ref[...])


def _ffn(x2d, g, w1, w3, w2):
    rows = x2d.shape[0]
    tm = min(FFN_ROW_TILE, rows)
    assert rows % tm == 0
    return pl.pallas_call(
        _ffn_body,
        grid=(rows // tm,),
        in_specs=[
            pl.BlockSpec((tm, D_MODEL), lambda i: (i, 0)),
            _const_spec((1, D_MODEL)),
            _const_spec((D_MODEL, D_FF)),
            _const_spec((D_MODEL, D_FF)),
            _const_spec((D_FF, D_MODEL)),
        ],
        out_specs=pl.BlockSpec((tm, D_MODEL), lambda i: (i, 0)),
        out_shape=jax.ShapeDtypeStruct(x2d.shape, F32),
        compiler_params=pltpu.CompilerParams(
            dimension_semantics=("arbitrary",), vmem_limit_bytes=FFN_VMEM_LIMIT),
        name="ffn",
    )(x2d, g, w1, w3, w2)


def _mixer_body(*refs, tt, n_pad, pos0, has_init, emit_state):
    n_state = len(STATE_SHAPES)
    n_in = N_MIXER_PARAMS + (n_state if has_init else 0)
    (x_ref, g_ref, win_ref, wqt_ref, wvt_ref, gb_ref, qg_ref, kg_ref, sink_ref,
     woa_ref, cw_ref, cb_ref, lng_ref, lnb_ref, woc_ref, qkw_ref, qkb_ref, ifb_ref,
     wom_ref, wout_ref, onesk_ref, band_ref) = refs[:N_MIXER_PARAMS]
    init_refs = refs[N_MIXER_PARAMS:n_in]
    o_ref = refs[n_in]
    n_out = 1 + (n_state if emit_state else 0)
    final_refs = refs[n_in + 1:n_in + n_out]
    (kext, vext_t, uext, ush, qkext, qm_s, km_s, vm_s, ht_s, bcum_s, r_s, ya_s,
     h_s, zg_s, c_s, n_s, m_s) = refs[n_in + n_out:]

    t_idx = pl.program_id(1)
    nblk = tt // BLOCK
    nchunk = tt // MLSTM_CHUNK
    mask_rows = pos0 < n_pad

    @pl.when(t_idx == 0)
    def _init_state():
        if has_init:
            k0, v0, u0, qk0, c0, n0, m0 = init_refs
            kext[0:BLOCK, :] = k0[...]
            vext_t[:, 0:BLOCK] = v0[...]
            uext[0:CONV_TAIL, :] = u0[...]
            qkext[0:QK_TAIL, :] = qk0[...]
            c_s[...] = c0[...]
            n_s[...] = n0[...]
            m_s[...] = m0[...]
        else:
            kext[0:BLOCK, :] = jnp.zeros((BLOCK, KV_WIDTH), BF16)
            vext_t[:, 0:BLOCK] = jnp.zeros((KV_WIDTH, BLOCK), BF16)
            uext[0:CONV_TAIL, :] = jnp.zeros((CONV_TAIL, CONV_WIDTH), F32)
            qkext[0:QK_TAIL, :] = jnp.zeros((QK_TAIL, 2 * MLSTM_WIDTH), F32)
            c_s[...] = jnp.zeros(c_s.shape, F32)
            n_s[...] = jnp.zeros(n_s.shape, F32)
            m_s[...] = jnp.full(m_s.shape, NEG, F32)

    t0 = t_idx * tt
    if mask_rows:
        valid_col = (lax.broadcasted_iota(jnp.int32, (tt, 1), 0) + (pos0 + t0)) >= n_pad

    def mask_pad_rows(v, fill):
        return jnp.where(valid_col, v, fill) if mask_rows else v

    x = x_ref[0]
    h = _rms_rows(x, g_ref[...]).astype(BF16)
    h_s[...] = h

    def proj(lo, hi):
        return _dot(h, win_ref[:, lo:hi])

    qt = _dot_nt(wqt_ref[...], h)
    vt = _dot_nt(wvt_ref[...], h)
    k = proj(OFF_AK, OFF_CA)
    k = k * lax.rsqrt(_seg_sumsq(k, onesk_ref) * (1.0 / HEAD_DIM) + RMS_EPS) * kg_ref[...]
    kext[BLOCK:BLOCK + tt, :] = k.astype(BF16)
    vext_t[:, BLOCK:BLOCK + tt] = vt.astype(BF16)

    qn = []
    for a in range(ATTN_HEADS):
        qa = qt[a * HEAD_DIM:(a + 1) * HEAD_DIM, :]
        inv = lax.rsqrt(jnp.mean(qa * qa, axis=0, keepdims=True) + RMS_EPS) * (HEAD_DIM ** -0.5)
        qn.append(qa * inv)
    qgain = qg_ref[...]
    band = band_ref[...]
    zeros_q = jnp.zeros((HEAD_DIM, ATTN_GROUP * BLOCK), BF16)
    key_iota = lax.broadcasted_iota(jnp.int32, (2 * BLOCK, ATTN_GROUP * BLOCK), 0)
    for n in range(nblk):
        r0 = n * BLOCK
        if pos0 + r0 - BLOCK >= n_pad:
            bias = band
        else:
            key_pos = key_iota + (pos0 + t0 + r0 - BLOCK)
            bias = jnp.where(key_pos >= n_pad, band, NEG)
        kcat = kext[r0:r0 + 2 * BLOCK, :]
        for hk in range(ATTN_KV_HEADS):
            qh = jnp.concatenate(
                [(qn[hk * ATTN_GROUP + g][:, r0:r0 + BLOCK] * qgain).astype(BF16)
                 for g in range(ATTN_GROUP)], axis=1)
            rhs = jnp.concatenate([qh, zeros_q] if hk == 0 else [zeros_q, qh], axis=0)
            st = _dot(kcat, rhs) + bias
            sink = sink_ref[hk:hk + 1, :]
            m = jnp.maximum(jnp.max(st, axis=0, keepdims=True), sink)
            p = jnp.exp(st - m)
            den = jnp.sum(p, axis=0, keepdims=True) + jnp.exp(sink - m)
            vth = vext_t[hk * HEAD_DIM:(hk + 1) * HEAD_DIM, r0:r0 + 2 * BLOCK]
            ot = _dot(vth, p.astype(BF16)) / den
            for j in range(ATTN_GROUP // 2):
                pair = jnp.concatenate(
                    [ot[:, (2 * j) * BLOCK:(2 * j + 1) * BLOCK],
                     ot[:, (2 * j + 1) * BLOCK:(2 * j + 2) * BLOCK]], axis=0)
                c0 = (hk * ATTN_GROUP + 2 * j) * HEAD_DIM
                ya_s[r0:r0 + BLOCK, c0:c0 + 2 * HEAD_DIM] = pair.T.astype(BF16)
    kext[0:BLOCK, :] = kext[tt:tt + BLOCK, :]
    vext_t[:, 0:BLOCK] = vext_t[:, tt:tt + BLOCK]
    ya = _dot(ya_s[...], woa_ref[...])

    ca = proj(OFF_CA, OFF_CG)
    cg = proj(OFF_CG, OFF_MQ)
    uext[CONV_TAIL:CONV_TAIL + tt, :] = mask_pad_rows(ca * jax.nn.sigmoid(cg), 0.0)
    for s in range(1, V7X_SUBLANES):
        ush[s - 1] = uext[s:s + tt + CONV_SHIFT_PAD, :]
    base = CONV_TAIL - (CONV_KERNEL - 1)
    acc = jnp.zeros((tt, CONV_WIDTH), F32) + cb_ref[...]
    for j in range(CONV_KERNEL):
        off = base + j
        s, a = off % V7X_SUBLANES, (off // V7X_SUBLANES) * V7X_SUBLANES
        tap = uext[a:a + tt, :] if s == 0 else ush[s - 1, a:a + tt, :]
        acc = acc + tap * cw_ref[j:j + 1, :]
    uext[0:CONV_TAIL, :] = uext[tt:tt + CONV_TAIL, :]
    mu = jnp.mean(acc, axis=-1, keepdims=True)
    xc = acc - mu
    yn = xc * lax.rsqrt(jnp.mean(xc * xc, axis=-1, keepdims=True) + LN_EPS)
    yn = yn * lng_ref[...] + lnb_ref[...]
    yc = _dot((yn * jax.nn.sigmoid(yn)).astype(BF16), woc_ref[...])

    mqk = proj(OFF_MQ, OFF_MV)
    qkext[QK_TAIL:QK_TAIL + tt, :] = mask_pad_rows(mqk, 0.0)
    qbase = QK_TAIL - (QK_CONV_KERNEL - 1)
    qacc = jnp.zeros((tt, 2 * MLSTM_WIDTH), F32) + qkb_ref[...]
    for j in range(QK_CONV_KERNEL):
        qacc = qacc + qkext[qbase + j:qbase + j + tt, :] * qkw_ref[j:j + 1, :]
    qkext[0:QK_TAIL, :] = qkext[tt:tt + QK_TAIL, :]
    qk = qacc * jax.nn.sigmoid(qacc)
    qm_s[...] = qk[:, :MLSTM_WIDTH] * (MLSTM_HEAD_DIM ** -0.5)
    km_s[...] = qk[:, MLSTM_WIDTH:]
    vm_s[...] = proj(OFF_MV, OFF_MO)

    zif = proj(OFF_IF, N_IN_PACKED) + ifb_ref[...]
    log_i = mask_pad_rows(zif, NEG)
    log_f = pltpu.roll(jax.nn.log_sigmoid(zif), V7X_LANES - MLSTM_HEADS, axis=1)
    pos = lax.broadcasted_iota(jnp.int32, (tt, 1), 0) % MLSTM_CHUNK
    bcum = log_f
    d = 1
    while d < MLSTM_CHUNK:
        bcum = bcum + jnp.where(pos >= d, pltpu.roll(bcum, d, axis=0), 0.0)
        d *= 2
    bcum_s[...] = bcum
    r_s[...] = log_i - bcum

    tri = (lax.broadcasted_iota(jnp.int32, (MLSTM_CHUNK, MLSTM_CHUNK), 0) >=
           lax.broadcasted_iota(jnp.int32, (MLSTM_CHUNK, MLSTM_CHUNK), 1))

    zg_slab = N_BRANCH * D_MODEL // nchunk
    assert zg_slab % V7X_LANES == 0

    def chunk_body(c, carry):
        z0 = pl.multiple_of(c * zg_slab, V7X_LANES)
        zg_s[:, pl.ds(z0, zg_slab)] = _dot(h_s[...], win_ref[:, pl.ds(OFF_ZG + z0, zg_slab)])

        r0 = pl.multiple_of(c * MLSTM_CHUNK, MLSTM_CHUNK)
        rows = pl.ds(r0, MLSTM_CHUNK)
        bc = bcum_s[rows, :]
        rc = r_s[rows, :]
        rct = rc.T
        for hd in range(MLSTM_HEADS):
            lanes = slice(hd * MLSTM_HEAD_DIM, (hd + 1) * MLSTM_HEAD_DIM)
            qc = qm_s[rows, lanes]
            kc = km_s[rows, lanes]
            vc = vm_s[rows, lanes]
            bcol = bc[:, hd:hd + 1]
            rcol = rc[:, hd:hd + 1]
            rrow = rct[hd:hd + 1, :]
            b_end = bc[MLSTM_CHUNK - 1:MLSTM_CHUNK, hd:hd + 1]
            m_prev = m_s[hd:hd + 1, 0:1]
            n_prev = n_s[hd:hd + 1, :]
            c_prev = c_s[hd]

            log_d = jnp.where(tri, bcol + rrow, NEG)
            m_inter = bcol + m_prev
            m_t = jnp.maximum(m_inter, jnp.max(log_d, axis=-1, keepdims=True))
            w_inter = jnp.exp(m_inter - m_t)
            qb = qc.astype(BF16)
            kb = kc.astype(BF16)
            vb = vc.astype(BF16)
            s = _dot_nt(qb, kb) * jnp.exp(log_d - m_t)
            num = _dot(s.astype(BF16), vb) + w_inter * _dot(qb, c_prev.astype(BF16))
            den = (jnp.sum(s, axis=-1, keepdims=True) +
                   w_inter * jnp.sum(qc * n_prev, axis=-1, keepdims=True))
            ht_s[rows, lanes] = num / jnp.maximum(jnp.abs(den), jnp.exp(-m_t))

            log_e = b_end + rcol
            m_loc = jnp.max(log_e, axis=0, keepdims=True)
            w_e = jnp.exp(log_e - m_loc)
            m_new = jnp.maximum(b_end + m_prev, m_loc)
            a = jnp.exp(b_end + m_prev - m_new)
            cc = jnp.exp(m_loc - m_new)
            kw = kc * w_e
            c_s[hd] = a * c_prev + cc * _dot_tn(kw.astype(BF16), vb)
            n_s[hd:hd + 1, :] = a * n_prev + cc * jnp.sum(kw, axis=0, keepdims=True)
            m_s[hd:hd + 1, :] = jnp.broadcast_to(m_new, (1, V7X_LANES))
        return carry

    lax.fori_loop(0, nchunk, chunk_body, 0)

    mo = proj(OFF_MO, OFF_ZG)
    ym = _dot(((0.5 + 0.5 * jnp.tanh(0.5 * mo)) * ht_s[...]).astype(BF16), wom_ref[...])

    def gate(i, y):
        zg = zg_s[:, i * D_MODEL:(i + 1) * D_MODEL] + gb_ref[i:i + 1, :]
        return (1.0 + jnp.tanh(0.5 * zg)) * y

    y = 0.5 * (gate(0, ya) + gate(1, yc) + gate(2, ym))
    o_ref[0] = x + _dot(y.astype(BF16), wout_ref[...])

    if emit_state:
        @pl.when(t_idx == pl.num_programs(1) - 1)
        def _emit_state():
            k1, v1, u1, qk1, c1, n1, m1 = final_refs
            k1[...] = kext[0:BLOCK, :]
            v1[...] = vext_t[:, 0:BLOCK]
            u1[...] = uext[0:CONV_TAIL, :]
            qk1[...] = qkext[0:QK_TAIL, :]
            c1[...] = c_s[...]
            n1[...] = n_s[...]
            m1[...] = m_s[...]


def _mixer(hseq, p, consts, *, n_pad, pos0, init_state=None, emit_state=False):
    bsz, t_len, _ = hseq.shape
    tt = min(MIXER_TIME_TILE, t_len)
    assert t_len % tt == 0 and tt % BLOCK == 0 and tt % MLSTM_CHUNK == 0
    assert tt >= CONV_TAIL and tt >= BLOCK
    assert not emit_state or bsz == 1
    has_init = init_state is not None
    nchunk = tt // MLSTM_CHUNK
    body = functools.partial(_mixer_body, tt=tt, n_pad=n_pad, pos0=pos0, has_init=has_init,
                             emit_state=emit_state)
    in_specs = [
        pl.BlockSpec((1, tt, D_MODEL), lambda b, t: (b, t, 0)),
        _const_spec((1, D_MODEL)),
        _const_spec((D_MODEL, N_IN_PACKED)),
        _const_spec((ATTN_WIDTH, D_MODEL)),
        _const_spec((KV_WIDTH, D_MODEL)),
        _const_spec((N_BRANCH, D_MODEL)),
        _const_spec((HEAD_DIM, BLOCK)),
        _const_spec((1, KV_WIDTH)),
        _const_spec((ATTN_KV_HEADS, ATTN_GROUP * BLOCK)),
        _const_spec((ATTN_WIDTH, D_MODEL)),
        _const_spec((CONV_KERNEL, CONV_WIDTH)),
        _const_spec((1, CONV_WIDTH)),
        _const_spec((1, CONV_WIDTH)),
        _const_spec((1, CONV_WIDTH)),
        _const_spec((CONV_WIDTH, D_MODEL)),
        _const_spec((QK_CONV_KERNEL, 2 * MLSTM_WIDTH)),
        _const_spec((1, 2 * MLSTM_WIDTH)),
        _const_spec((1, V7X_LANES)),
        _const_spec((MLSTM_WIDTH, D_MODEL)),
        _const_spec((D_MODEL, D_MODEL)),
        _const_spec((KV_WIDTH, KV_WIDTH)),
        _const_spec((2 * BLOCK, ATTN_GROUP * BLOCK)),
    ]
    args = [hseq, p["norm"], p["w_in"], p["w_q_t"], p["w_v_t"], p["gate_bias"], p["q_gain"],
            p["k_gain"], p["sinks"], p["w_o_attn"], p["conv_w"], p["conv_b"], p["ln_g"], p["ln_b"],
            p["w_o_conv"], p["qk_w"], p["qk_b"], p["if_bias"], p["w_o_mlstm"], p["w_out"],
            consts["ones_k"], consts["band"]]
    assert len(in_specs) == N_MIXER_PARAMS and len(args) == N_MIXER_PARAMS
    if has_init:
        in_specs += [_const_spec(shape) for shape, _ in STATE_SHAPES]
        args += list(init_state)
    out_specs = [pl.BlockSpec((1, tt, D_MODEL), lambda b, t: (b, t, 0))]
    out_shape = [jax.ShapeDtypeStruct(hseq.shape, F32)]
    if emit_state:
        for shape, dtype in STATE_SHAPES:
            out_specs.append(pl.BlockSpec(shape, lambda b, t, nd=len(shape): (0,) * nd))
            out_shape.append(jax.ShapeDtypeStruct(shape, dtype))
    scratch = [
        pltpu.VMEM((BLOCK + tt, KV_WIDTH), BF16),
        pltpu.VMEM((KV_WIDTH, BLOCK + tt), BF16),
        pltpu.VMEM((CONV_TAIL + tt, CONV_WIDTH), F32),
        pltpu.VMEM((V7X_SUBLANES - 1, tt + CONV_SHIFT_PAD, CONV_WIDTH), F32),
        pltpu.VMEM((QK_TAIL + tt, 2 * MLSTM_WIDTH), F32),
        pltpu.VMEM((tt, MLSTM_WIDTH), F32),
        pltpu.VMEM((tt, MLSTM_WIDTH), F32),
        pltpu.VMEM((tt, MLSTM_WIDTH), F32),
        pltpu.VMEM((tt, MLSTM_WIDTH), F32),
        pltpu.VMEM((tt, V7X_LANES), F32),
        pltpu.VMEM((tt, V7X_LANES), F32),
        pltpu.VMEM((tt, ATTN_WIDTH), BF16),
        pltpu.VMEM((tt, D_MODEL), BF16),
        pltpu.VMEM((tt, N_BRANCH * D_MODEL), F32),
        pltpu.VMEM((MLSTM_HEADS, MLSTM_HEAD_DIM, MLSTM_HEAD_DIM), F32),
        pltpu.VMEM((V7X_SUBLANES, MLSTM_HEAD_DIM), F32),
        pltpu.VMEM((V7X_SUBLANES, V7X_LANES), F32),
    ]
    outs = pl.pallas_call(
        body,
        grid=(bsz, t_len // tt),
        in_specs=in_specs,
        out_specs=out_specs,
        out_shape=out_shape,
        scratch_shapes=scratch,
        compiler_params=pltpu.CompilerParams(
            dimension_semantics=("arbitrary", "arbitrary"),
            vmem_limit_bytes=MIXER_VMEM_LIMIT),
        name="mixer_lead" if emit_state else "mixer",
    )(*args)
    return (outs[0], tuple(outs[1:])) if emit_state else outs[0]


def _make_consts():
    i = jnp.arange(KV_WIDTH) // HEAD_DIM
    kj = jnp.arange(2 * BLOCK)
    qi = jnp.arange(ATTN_GROUP * BLOCK) % BLOCK
    allowed = (kj[:, None] > qi[None, :]) & (kj[:, None] <= qi[None, :] + BLOCK)
    return {
        "ones_k": (i[:, None] == i[None, :]).astype(BF16),
        "band": jnp.where(allowed, 0.0, NEG).astype(F32),
    }


def _pack_w_in(w_in):
    pad = jnp.zeros((D_MODEL, V7X_LANES - 2 * MLSTM_HEADS), w_in.dtype)
    return jnp.concatenate(
        [w_in[:, ORIG_OFF_AK:ORIG_OFF_AV], w_in[:, ORIG_OFF_CA:ORIG_OFF_IF], w_in[:, ORIG_OFF_ZG:],
         w_in[:, ORIG_OFF_IF:ORIG_OFF_ZG], pad], axis=1).astype(BF16)


def kernel(x, meta_tokens, ffn1_norm, ffn1_w1, ffn1_w3, ffn1_w2, mix_norm, w_in, gate_bias,
           attn_q_norm, attn_k_norm, attn_sinks, w_o_attn, conv_dw_w, conv_dw_b, conv_ln_g,
           conv_ln_b, w_o_conv, mlstm_qk_conv_w, mlstm_qk_conv_b, mlstm_igate_bias,
           mlstm_fgate_bias, w_o_mlstm, w_out, ffn2_norm, ffn2_w1, ffn2_w3, ffn2_w2):
    bsz, seq, _ = x.shape
    depth = w_in.shape[0]
    consts = _make_consts()

    def row(v):
        return v.reshape(1, -1).astype(F32)

    lead = jnp.concatenate(
        [jnp.zeros((N_PAD, D_MODEL), x.dtype), meta_tokens.astype(x.dtype)], axis=0)
    h = x

    for l in range(depth):
        if_bias = jnp.concatenate([
            mlstm_igate_bias[l], mlstm_fgate_bias[l],
            jnp.zeros((V7X_LANES - 2 * MLSTM_HEADS,), F32)]).reshape(1, V7X_LANES)
        mix = {
            "norm": row(mix_norm[l]),
            "w_in": _pack_w_in(w_in[l]),
            "w_q_t": w_in[l][:, ORIG_OFF_AQ:ORIG_OFF_AK].T.astype(BF16),
            "w_v_t": w_in[l][:, ORIG_OFF_AV:ORIG_OFF_CA].T.astype(BF16),
            "gate_bias": gate_bias[l].astype(F32),
            "q_gain": jnp.broadcast_to(attn_q_norm[l].astype(F32)[:, None], (HEAD_DIM, BLOCK)),
            "k_gain": row(jnp.tile(attn_k_norm[l], ATTN_KV_HEADS)),
            "sinks": jnp.repeat(attn_sinks[l].astype(F32), BLOCK).reshape(
                ATTN_KV_HEADS, ATTN_GROUP * BLOCK),
            "w_o_attn": w_o_attn[l].astype(BF16),
            "conv_w": conv_dw_w[l].astype(F32),
            "conv_b": row(conv_dw_b[l]),
            "ln_g": row(conv_ln_g[l]),
            "ln_b": row(conv_ln_b[l]),
            "w_o_conv": w_o_conv[l].astype(BF16),
            "qk_w": mlstm_qk_conv_w[l].astype(F32),
            "qk_b": row(mlstm_qk_conv_b[l]),
            "if_bias": if_bias,
            "w_o_mlstm": w_o_mlstm[l].astype(BF16),
            "w_out": w_out[l].astype(BF16),
        }
        ffn1 = (row(ffn1_norm[l]), ffn1_w1[l].astype(BF16), ffn1_w3[l].astype(BF16),
                ffn1_w2[l].astype(BF16))
        ffn2 = (row(ffn2_norm[l]), ffn2_w1[l].astype(BF16), ffn2_w3[l].astype(BF16),
                ffn2_w2[l].astype(BF16))

        lead = _ffn(lead, *ffn1)
        lead, state = _mixer(lead.reshape(1, LEAD, D_MODEL), mix, consts, n_pad=N_PAD, pos0=0,
                             emit_state=True)
        lead = lead.reshape(LEAD, D_MODEL)
        if l + 1 < depth:
            lead = _ffn(lead, *ffn2)

        h2 = _ffn(h.reshape(bsz * seq, D_MODEL), *ffn1)
        h = _mixer(h2.reshape(bsz, seq, D_MODEL), mix, consts, n_pad=N_PAD, pos0=LEAD,
                   init_state=state)
        h2 = _ffn(h.reshape(bsz * seq, D_MODEL), *ffn2)
        h = h2.reshape(bsz, seq, D_MODEL)
    return h
```

```python
import functools

import jax
import jax.numpy as jnp
from jax import lax
from jax.experimental import pallas as pl
from jax.experimental.pallas import tpu as pltpu

D_MODEL = 1024
N_META = 16
BLOCK = 128
LEAD = BLOCK
N_PAD = LEAD - N_META
ATTN_HEADS = 8
ATTN_KV_HEADS = 2
HEAD_DIM = 64
ATTN_GROUP = ATTN_HEADS // ATTN_KV_HEADS
ATTN_WIDTH = ATTN_HEADS * HEAD_DIM
KV_WIDTH = ATTN_KV_HEADS * HEAD_DIM
CONV_WIDTH = 512
CONV_KERNEL = 31
MLSTM_HEADS = 4
MLSTM_WIDTH = 512
MLSTM_HEAD_DIM = MLSTM_WIDTH // MLSTM_HEADS
MLSTM_CHUNK = 64
QK_CONV_KERNEL = 4
D_FF = 2816
N_BRANCH = 3
RMS_EPS = 1e-6
LN_EPS = 1e-5
NEG = -1e30

V7X_LANES = 128
V7X_SUBLANES = 8

ORIG_OFF_AQ = 0
ORIG_OFF_AK = ORIG_OFF_AQ + ATTN_WIDTH
ORIG_OFF_AV = ORIG_OFF_AK + KV_WIDTH
ORIG_OFF_CA = ORIG_OFF_AV + KV_WIDTH
ORIG_OFF_IF = ORIG_OFF_CA + 2 * CONV_WIDTH + 4 * MLSTM_WIDTH
ORIG_OFF_ZG = ORIG_OFF_IF + 2 * MLSTM_HEADS

OFF_AK = 0
OFF_CA = OFF_AK + KV_WIDTH
OFF_CG = OFF_CA + CONV_WIDTH
OFF_MQ = OFF_CG + CONV_WIDTH
OFF_MK = OFF_MQ + MLSTM_WIDTH
OFF_MV = OFF_MK + MLSTM_WIDTH
OFF_MO = OFF_MV + MLSTM_WIDTH
OFF_ZG = OFF_MO + MLSTM_WIDTH
OFF_IF = OFF_ZG + N_BRANCH * D_MODEL
N_IN_PACKED = OFF_IF + V7X_LANES

CONV_TAIL = 32
CONV_SHIFT_PAD = CONV_TAIL - V7X_SUBLANES
QK_TAIL = 8

FFN_ROW_TILE = 512
MIXER_TIME_TILE = 512
FFN_VMEM_LIMIT = 56 * 1024 * 1024
MIXER_VMEM_LIMIT = 60 * 1024 * 1024

F32 = jnp.float32
BF16 = jnp.bfloat16

STATE_SHAPES = (
    ((BLOCK, KV_WIDTH), BF16),
    ((KV_WIDTH, BLOCK), BF16),
    ((CONV_TAIL, CONV_WIDTH), F32),
    ((QK_TAIL, 2 * MLSTM_WIDTH), F32),
    ((MLSTM_HEADS, MLSTM_HEAD_DIM, MLSTM_HEAD_DIM), F32),
    ((V7X_SUBLANES, MLSTM_HEAD_DIM), F32),
    ((V7X_SUBLANES, V7X_LANES), F32),
)
N_MIXER_PARAMS = 22


def _const_spec(shape):
    nd = len(shape)
    return pl.BlockSpec(shape, lambda *_: (0,) * nd, pipeline_mode=pl.Buffered(1))


def _dot(a, b):
    return jnp.dot(a, b, preferred_element_type=F32)


def _dot_nt(a, b):
    return lax.dot_general(a, b, (((1,), (1,)), ((), ())), preferred_element_type=F32)


def _dot_tn(a, b):
    return lax.dot_general(a, b, (((0,), (0,)), ((), ())), preferred_element_type=F32)


def _rms_rows(x, gain):
    ms = jnp.mean(x * x, axis=-1, keepdims=True)
    return x * lax.rsqrt(ms + RMS_EPS) * gain


def _seg_sumsq(x, ones_ref):
    x2 = x * x
    hi = x2.astype(BF16)
    lo = (x2 - hi.astype(F32)).astype(BF16)
    ones = ones_ref[...]
    return _dot(hi, ones) + _dot(lo, ones)


def _ffn_body(x_ref, g_ref, w1_ref, w3_ref, w2_ref, o_ref):
    x = x_ref[...]
    h = _rms_rows(x, g_ref[...]).astype(BF16)
    a = _dot(h, w1_ref[...])
    b = _dot(h, w3_ref[...])
    u = (a * jax.nn.sigmoid(a) * b).astype(BF16)
    o_ref[...] = x + 0.5 * _dot(u, w2_ref[...])


def _ffn(x2d, g, w1, w3, w2):
    rows = x2d.shape[0]
    tm = min(FFN_ROW_TILE, rows)
    assert rows % tm == 0
    return pl.pallas_call(
        _ffn_body,
        grid=(rows // tm,),
        in_specs=[
            pl.BlockSpec((tm, D_MODEL), lambda i: (i, 0)),
            _const_spec((1, D_MODEL)),
            _const_spec((D_MODEL, D_FF)),
            _const_spec((D_MODEL, D_FF)),
            _const_spec((D_FF, D_MODEL)),
        ],
        out_specs=pl.BlockSpec((tm, D_MODEL), lambda i: (i, 0)),
        out_shape=jax.ShapeDtypeStruct(x2d.shape, F32),
        compiler_params=pltpu.CompilerParams(
            dimension_semantics=("arbitrary",), vmem_limit_bytes=FFN_VMEM_LIMIT),
        name="ffn",
    )(x2d, g, w1, w3, w2)


def _mixer_body(*refs, tt, n_pad, pos0, has_init, emit_state):
    n_state = len(STATE_SHAPES)
    n_in = N_MIXER_PARAMS + (n_state if has_init else 0)
    (x_ref, g_ref, win_ref, wqt_ref, wvt_ref, gb_ref, qg_ref, kg_ref, sink_ref,
     woa_ref, cw_ref, cb_ref, lng_ref, lnb_ref, woc_ref, qkw_ref, qkb_ref, ifb_ref,
     wom_ref, wout_ref, onesk_ref, band_ref) = refs[:N_MIXER_PARAMS]
    init_refs = refs[N_MIXER_PARAMS:n_in]
    o_ref = refs[n_in]
    n_out = 1 + (n_state if emit_state else 0)
    final_refs = refs[n_in + 1:n_in + n_out]
    (kext, vext_t, uext, ush, qkext, qm_s, km_s, vm_s, ht_s, bcum_s, r_s, ya_s,
     h_s, zg_s, c_s, n_s, m_s) = refs[n_in + n_out:]

    t_idx = pl.program_id(1)
    nblk = tt // BLOCK
    nchunk = tt // MLSTM_CHUNK
    mask_rows = pos0 < n_pad

    @pl.when(t_idx == 0)
    def _init_state():
        if has_init:
            k0, v0, u0, qk0, c0, n0, m0 = init_refs
            kext[0:BLOCK, :] = k0[...]
            vext_t[:, 0:BLOCK] = v0[...]
            uext[0:CONV_TAIL, :] = u0[...]
            qkext[0:QK_TAIL, :] = qk0[...]
            c_s[...] = c0[...]
            n_s[...] = n0[...]
            m_s[...] = m0[...]
        else:
            kext[0:BLOCK, :] = jnp.zeros((BLOCK, KV_WIDTH), BF16)
            vext_t[:, 0:BLOCK] = jnp.zeros((KV_WIDTH, BLOCK), BF16)
            uext[0:CONV_TAIL, :] = jnp.zeros((CONV_TAIL, CONV_WIDTH), F32)
            qkext[0:QK_TAIL, :] = jnp.zeros((QK_TAIL, 2 * MLSTM_WIDTH), F32)
            c_s[...] = jnp.zeros(c_s.shape, F32)
            n_s[...] = jnp.zeros(n_s.shape, F32)
            m_s[...] = jnp.full(m_s.shape, NEG, F32)

    t0 = t_idx * tt
    if mask_rows:
        valid_col = (lax.broadcasted_iota(jnp.int32, (tt, 1), 0) + (pos0 + t0)) >= n_pad

    def mask_pad_rows(v, fill):
        return jnp.where(valid_col, v, fill) if mask_rows else v

    x = x_ref[0]
    h = _rms_rows(x, g_ref[...]).astype(BF16)
    h_s[...] = h

    def proj(lo, hi):
        return _dot(h, win_ref[:, lo:hi])

    qt = _dot_nt(wqt_ref[...], h)
    vt = _dot_nt(wvt_ref[...], h)
    k = proj(OFF_AK, OFF_CA)
    k = k * lax.rsqrt(_seg_sumsq(k, onesk_ref) * (1.0 / HEAD_DIM) + RMS_EPS) * kg_ref[...]
    kext[BLOCK:BLOCK + tt, :] = k.astype(BF16)
    vext_t[:, BLOCK:BLOCK + tt] = vt.astype(BF16)

    qn = []
    for a in range(ATTN_HEADS):
        qa = qt[a * HEAD_DIM:(a + 1) * HEAD_DIM, :]
        inv = lax.rsqrt(jnp.mean(qa * qa, axis=0, keepdims=True) + RMS_EPS) * (HEAD_DIM ** -0.5)
        qn.append(qa * inv)
    qgain = qg_ref[...]
    band = band_ref[...]
    zeros_q = jnp.zeros((HEAD_DIM, ATTN_GROUP * BLOCK), BF16)
    key_iota = lax.broadcasted_iota(jnp.int32, (2 * BLOCK, ATTN_GROUP * BLOCK), 0)
    for n in range(nblk):
        r0 = n * BLOCK
        if pos0 + r0 - BLOCK >= n_pad:
            bias = band
        else:
            key_pos = key_iota + (pos0 + t0 + r0 - BLOCK)
            bias = jnp.where(key_pos >= n_pad, band, NEG)
        kcat = kext[r0:r0 + 2 * BLOCK, :]
        for hk in range(ATTN_KV_HEADS):
            qh = jnp.concatenate(
                [(qn[hk * ATTN_GROUP + g][:, r0:r0 + BLOCK] * qgain).astype(BF16)
                 for g in range(ATTN_GROUP)], axis=1)
            rhs = jnp.concatenate([qh, zeros_q] if hk == 0 else [zeros_q, qh], axis=0)
            st = _dot(kcat, rhs) + bias
            sink = sink_ref[hk:hk + 1, :]
            m = jnp.maximum(jnp.max(st, axis=0, keepdims=True), sink)
            p = jnp.exp(st - m)
            den = jnp.sum(p, axis=0, keepdims=True) + jnp.exp(sink - m)
            vth = vext_t[hk * HEAD_DIM:(hk + 1) * HEAD_DIM, r0:r0 + 2 * BLOCK]
            ot = _dot(vth, p.astype(BF16)) / den
            for j in range(ATTN_GROUP // 2):
                pair = jnp.concatenate(
                    [ot[:, (2 * j) * BLOCK:(2 * j + 1) * BLOCK],
                     ot[:, (2 * j + 1) * BLOCK:(2 * j + 2) * BLOCK]], axis=0)
                c0 = (hk * ATTN_GROUP + 2 * j) * HEAD_DIM
                ya_s[r0:r0 + BLOCK, c0:c0 + 2 * HEAD_DIM] = pair.T.astype(BF16)
    kext[0:BLOCK, :] = kext[tt:tt + BLOCK, :]
    vext_t[:, 0:BLOCK] = vext_t[:, tt:tt + BLOCK]
    ya = _dot(ya_s[...], woa_ref[...])

    ca = proj(OFF_CA, OFF_CG)
    cg = proj(OFF_CG, OFF_MQ)
    uext[CONV_TAIL:CONV_TAIL + tt, :] = mask_pad_rows(ca * jax.nn.sigmoid(cg), 0.0)
    for s in range(1, V7X_SUBLANES):
        ush[s - 1] = uext[s:s + tt + CONV_SHIFT_PAD, :]
    base = CONV_TAIL - (CONV_KERNEL - 1)
    acc = jnp.zeros((tt, CONV_WIDTH), F32) + cb_ref[...]
    for j in range(CONV_KERNEL):
        off = base + j
        s, a = off % V7X_SUBLANES, (off // V7X_SUBLANES) * V7X_SUBLANES
        tap = uext[a:a + tt, :] if s == 0 else ush[s - 1, a:a + tt, :]
        acc = acc + tap * cw_ref[j:j + 1, :]
    uext[0:CONV_TAIL, :] = uext[tt:tt + CONV_TAIL, :]
    mu = jnp.mean(acc, axis=-1, keepdims=True)
    xc = acc - mu
    yn = xc * lax.rsqrt(jnp.mean(xc * xc, axis=-1, keepdims=True) + LN_EPS)
    yn = yn * lng_ref[...] + lnb_ref[...]
    yc = _dot((yn * jax.nn.sigmoid(yn)).astype(BF16), woc_ref[...])

    mqk = proj(OFF_MQ, OFF_MV)
    qkext[QK_TAIL:QK_TAIL + tt, :] = mask_pad_rows(mqk, 0.0)
    qbase = QK_TAIL - (QK_CONV_KERNEL - 1)
    qacc = jnp.zeros((tt, 2 * MLSTM_WIDTH), F32) + qkb_ref[...]
    for j in range(QK_CONV_KERNEL):
        qacc = qacc + qkext[qbase + j:qbase + j + tt, :] * qkw_ref[j:j + 1, :]
    qkext[0:QK_TAIL, :] = qkext[tt:tt + QK_TAIL, :]
    qk = qacc * jax.nn.sigmoid(qacc)
    qm_s[...] = qk[:, :MLSTM_WIDTH] * (MLSTM_HEAD_DIM ** -0.5)
    km_s[...] = qk[:, MLSTM_WIDTH:]
    vm_s[...] = proj(OFF_MV, OFF_MO)

    zif = proj(OFF_IF, N_IN_PACKED) + ifb_ref[...]
    log_i = mask_pad_rows(zif, NEG)
    log_f = pltpu.roll(jax.nn.log_sigmoid(zif), V7X_LANES - MLSTM_HEADS, axis=1)
    pos = lax.broadcasted_iota(jnp.int32, (tt, 1), 0) % MLSTM_CHUNK
    bcum = log_f
    d = 1
    while d < MLSTM_CHUNK:
        bcum = bcum + jnp.where(pos >= d, pltpu.roll(bcum, d, axis=0), 0.0)
        d *= 2
    bcum_s[...] = bcum
    r_s[...] = log_i - bcum

    tri = (lax.broadcasted_iota(jnp.int32, (MLSTM_CHUNK, MLSTM_CHUNK), 0) >=
           lax.broadcasted_iota(jnp.int32, (MLSTM_CHUNK, MLSTM_CHUNK), 1))

    zg_slab = N_BRANCH * D_MODEL // nchunk
    assert zg_slab % V7X_LANES == 0

    def chunk_body(c, carry):
        def gate_project():
            z0 = pl.multiple_of(c * zg_slab, V7X_LANES)
            zg_s[:, pl.ds(z0, zg_slab)] = _dot(h_s[...], win_ref[:, pl.ds(OFF_ZG + z0, zg_slab)])

        r0 = pl.multiple_of(c * MLSTM_CHUNK, MLSTM_CHUNK)
        rows = pl.ds(r0, MLSTM_CHUNK)
        bc = bcum_s[rows, :]
        rc = r_s[rows, :]
        rct = rc.T
        for hd in range(MLSTM_HEADS):
            lanes = slice(hd * MLSTM_HEAD_DIM, (hd + 1) * MLSTM_HEAD_DIM)
            qc = qm_s[rows, lanes]
            kc = km_s[rows, lanes]
            vc = vm_s[rows, lanes]
            bcol = bc[:, hd:hd + 1]
            rcol = rc[:, hd:hd + 1]
            rrow = rct[hd:hd + 1, :]
            b_end = bc[MLSTM_CHUNK - 1:MLSTM_CHUNK, hd:hd + 1]
            m_prev = m_s[hd:hd + 1, 0:1]
            n_prev = n_s[hd:hd + 1, :]
            c_prev = c_s[hd]

            log_d = jnp.where(tri, bcol + rrow, NEG)
            m_inter = bcol + m_prev
            m_t = jnp.maximum(m_inter, jnp.max(log_d, axis=-1, keepdims=True))
            w_inter = jnp.exp(m_inter - m_t)
            qb = qc.astype(BF16)
            kb = kc.astype(BF16)
            vb = vc.astype(BF16)
            scores = _dot_nt(qb, kb)
            carried = _dot(qb, c_prev.astype(BF16))
            if hd == 0:
                gate_project()
            s = scores * jnp.exp(log_d - m_t)
            num = _dot(s.astype(BF16), vb) + w_inter * carried
            den = (jnp.sum(s, axis=-1, keepdims=True) +
                   w_inter * jnp.sum(qc * n_prev, axis=-1, keepdims=True))
            ht_s[rows, lanes] = num / jnp.maximum(jnp.abs(den), jnp.exp(-m_t))

            log_e = b_end + rcol
            m_loc = jnp.max(log_e, axis=0, keepdims=True)
            w_e = jnp.exp(log_e - m_loc)
            m_new = jnp.maximum(b_end + m_prev, m_loc)
            a = jnp.exp(b_end + m_prev - m_new)
            cc = jnp.exp(m_loc - m_new)
            kw = kc * w_e
            c_s[hd] = a * c_prev + cc * _dot_tn(kw.astype(BF16), vb)
            n_s[hd:hd + 1, :] = a * n_prev + cc * jnp.sum(kw, axis=0, keepdims=True)
            m_s[hd:hd + 1, :] = jnp.broadcast_to(m_new, (1, V7X_LANES))
        return carry

    lax.fori_loop(0, nchunk, chunk_body, 0)

    mo = proj(OFF_MO, OFF_ZG)
    ym = _dot(((0.5 + 0.5 * jnp.tanh(0.5 * mo)) * ht_s[...]).astype(BF16), wom_ref[...])

    def gate(i, y):
        zg = zg_s[:, i * D_MODEL:(i + 1) * D_MODEL] + gb_ref[i:i + 1, :]
        return (1.0 + jnp.tanh(0.5 * zg)) * y

    y = 0.5 * (gate(0, ya) + gate(1, yc) + gate(2, ym))
    o_ref[0] = x + _dot(y.astype(BF16), wout_ref[...])

    if emit_state:
        @pl.when(t_idx == pl.num_programs(1) - 1)
        def _emit_state():
            k1, v1, u1, qk1, c1, n1, m1 = final_refs
            k1[...] = kext[0:BLOCK, :]
            v1[...] = vext_t[:, 0:BLOCK]
            u1[...] = uext[0:CONV_TAIL, :]
            qk1[...] = qkext[0:QK_TAIL, :]
            c1[...] = c_s[...]
            n1[...] = n_s[...]
            m1[...] = m_s[...]


def _mixer(hseq, p, consts, *, n_pad, pos0, init_state=None, emit_state=False):
    bsz, t_len, _ = hseq.shape
    tt = min(MIXER_TIME_TILE, t_len)
    assert t_len % tt == 0 and tt % BLOCK == 0 and tt % MLSTM_CHUNK == 0
    assert tt >= CONV_TAIL and tt >= BLOCK
    assert not emit_state or bsz == 1
    has_init = init_state is not None
    nchunk = tt // MLSTM_CHUNK
    body = functools.partial(_mixer_body, tt=tt, n_pad=n_pad, pos0=pos0, has_init=has_init,
                             emit_state=emit_state)
    in_specs = [
        pl.BlockSpec((1, tt, D_MODEL), lambda b, t: (b, t, 0)),
        _const_spec((1, D_MODEL)),
        _const_spec((D_MODEL, N_IN_PACKED)),
        _const_spec((ATTN_WIDTH, D_MODEL)),
        _const_spec((KV_WIDTH, D_MODEL)),
        _const_spec((N_BRANCH, D_MODEL)),
        _const_spec((HEAD_DIM, BLOCK)),
        _const_spec((1, KV_WIDTH)),
        _const_spec((ATTN_KV_HEADS, ATTN_GROUP * BLOCK)),
        _const_spec((ATTN_WIDTH, D_MODEL)),
        _const_spec((CONV_KERNEL, CONV_WIDTH)),
        _const_spec((1, CONV_WIDTH)),
        _const_spec((1, CONV_WIDTH)),
        _const_spec((1, CONV_WIDTH)),
        _const_spec((CONV_WIDTH, D_MODEL)),
        _const_spec((QK_CONV_KERNEL, 2 * MLSTM_WIDTH)),
        _const_spec((1, 2 * MLSTM_WIDTH)),
        _const_spec((1, V7X_LANES)),
        _const_spec((MLSTM_WIDTH, D_MODEL)),
        _const_spec((D_MODEL, D_MODEL)),
        _const_spec((KV_WIDTH, KV_WIDTH)),
        _const_spec((2 * BLOCK, ATTN_GROUP * BLOCK)),
    ]
    args = [hseq, p["norm"], p["w_in"], p["w_q_t"], p["w_v_t"], p["gate_bias"], p["q_gain"],
            p["k_gain"], p["sinks"], p["w_o_attn"], p["conv_w"], p["conv_b"], p["ln_g"], p["ln_b"],
            p["w_o_conv"], p["qk_w"], p["qk_b"], p["if_bias"], p["w_o_mlstm"], p["w_out"],
            consts["ones_k"], consts["band"]]
    assert len(in_specs) == N_MIXER_PARAMS and len(args) == N_MIXER_PARAMS
    if has_init:
        in_specs += [_const_spec(shape) for shape, _ in STATE_SHAPES]
        args += list(init_state)
    out_specs = [pl.BlockSpec((1, tt, D_MODEL), lambda b, t: (b, t, 0))]
    out_shape = [jax.ShapeDtypeStruct(hseq.shape, F32)]
    if emit_state:
        for shape, dtype in STATE_SHAPES:
            out_specs.append(pl.BlockSpec(shape, lambda b, t, nd=len(shape): (0,) * nd))
            out_shape.append(jax.ShapeDtypeStruct(shape, dtype))
    scratch = [
        pltpu.VMEM((BLOCK + tt, KV_WIDTH), BF16),
        pltpu.VMEM((KV_WIDTH, BLOCK + tt), BF16),
        pltpu.VMEM((CONV_TAIL + tt, CONV_WIDTH), F32),
        pltpu.VMEM((V7X_SUBLANES - 1, tt + CONV_SHIFT_PAD, CONV_WIDTH), F32),
        pltpu.VMEM((QK_TAIL + tt, 2 * MLSTM_WIDTH), F32),
        pltpu.VMEM((tt, MLSTM_WIDTH), F32),
        pltpu.VMEM((tt, MLSTM_WIDTH), F32),
        pltpu.VMEM((tt, MLSTM_WIDTH), F32),
        pltpu.VMEM((tt, MLSTM_WIDTH), F32),
        pltpu.VMEM((tt, V7X_LANES), F32),
        pltpu.VMEM((tt, V7X_LANES), F32),
        pltpu.VMEM((tt, ATTN_WIDTH), BF16),
        pltpu.VMEM((tt, D_MODEL), BF16),
        pltpu.VMEM((tt, N_BRANCH * D_MODEL), F32),
        pltpu.VMEM((MLSTM_HEADS, MLSTM_HEAD_DIM, MLSTM_HEAD_DIM), F32),
        pltpu.VMEM((V7X_SUBLANES, MLSTM_HEAD_DIM), F32),
        pltpu.VMEM((V7X_SUBLANES, V7X_LANES), F32),
    ]
    outs = pl.pallas_call(
        body,
        grid=(bsz, t_len // tt),
        in_specs=in_specs,
        out_specs=out_specs,
        out_shape=out_shape,
        scratch_shapes=scratch,
        compiler_params=pltpu.CompilerParams(
            dimension_semantics=("arbitrary", "arbitrary"),
            vmem_limit_bytes=MIXER_VMEM_LIMIT),
        name="mixer_lead" if emit_state else "mixer",
    )(*args)
    return (outs[0], tuple(outs[1:])) if emit_state else outs[0]


def _make_consts():
    i = jnp.arange(KV_WIDTH) // HEAD_DIM
    kj = jnp.arange(2 * BLOCK)
    qi = jnp.arange(ATTN_GROUP * BLOCK) % BLOCK
    allowed = (kj[:, None] > qi[None, :]) & (kj[:, None] <= qi[None, :] + BLOCK)
    return {
        "ones_k": (i[:, None] == i[None, :]).astype(BF16),
        "band": jnp.where(allowed, 0.0, NEG).astype(F32),
    }


def _pack_w_in(w_in):
    pad = jnp.zeros((D_MODEL, V7X_LANES - 2 * MLSTM_HEADS), w_in.dtype)
    return jnp.concatenate(
        [w_in[:, ORIG_OFF_AK:ORIG_OFF_AV], w_in[:, ORIG_OFF_CA:ORIG_OFF_IF], w_in[:, ORIG_OFF_ZG:],
         w_in[:, ORIG_OFF_IF:ORIG_OFF_ZG], pad], axis=1).astype(BF16)


def kernel(x, meta_tokens, ffn1_norm, ffn1_w1, ffn1_w3, ffn1_w2, mix_norm, w_in, gate_bias,
           attn_q_norm, attn_k_norm, attn_sinks, w_o_attn, conv_dw_w, conv_dw_b, conv_ln_g,
           conv_ln_b, w_o_conv, mlstm_qk_conv_w, mlstm_qk_conv_b, mlstm_igate_bias,
           mlstm_fgate_bias, w_o_mlstm, w_out, ffn2_norm, ffn2_w1, ffn2_w3, ffn2_w2):
    bsz, seq, _ = x.shape
    depth = w_in.shape[0]
    consts = _make_consts()

    def row(v):
        return v.reshape(1, -1).astype(F32)

    lead = jnp.concatenate(
        [jnp.zeros((N_PAD, D_MODEL), x.dtype), meta_tokens.astype(x.dtype)], axis=0)
    h = x

    for l in range(depth):
        if_bias = jnp.concatenate([
            mlstm_igate_bias[l], mlstm_fgate_bias[l],
            jnp.zeros((V7X_LANES - 2 * MLSTM_HEADS,), F32)]).reshape(1, V7X_LANES)
        mix = {
            "norm": row(mix_norm[l]),
            "w_in": _pack_w_in(w_in[l]),
            "w_q_t": w_in[l][:, ORIG_OFF_AQ:ORIG_OFF_AK].T.astype(BF16),
            "w_v_t": w_in[l][:, ORIG_OFF_AV:ORIG_OFF_CA].T.astype(BF16),
            "gate_bias": gate_bias[l].astype(F32),
            "q_gain": jnp.broadcast_to(attn_q_norm[l].astype(F32)[:, None], (HEAD_DIM, BLOCK)),
            "k_gain": row(jnp.tile(attn_k_norm[l], ATTN_KV_HEADS)),
            "sinks": jnp.repeat(attn_sinks[l].astype(F32), BLOCK).reshape(
                ATTN_KV_HEADS, ATTN_GROUP * BLOCK),
            "w_o_attn": w_o_attn[l].astype(BF16),
            "conv_w": conv_dw_w[l].astype(F32),
            "conv_b": row(conv_dw_b[l]),
            "ln_g": row(conv_ln_g[l]),
            "ln_b": row(conv_ln_b[l]),
            "w_o_conv": w_o_conv[l].astype(BF16),
            "qk_w": mlstm_qk_conv_w[l].astype(F32),
            "qk_b": row(mlstm_qk_conv_b[l]),
            "if_bias": if_bias,
            "w_o_mlstm": w_o_mlstm[l].astype(BF16),
            "w_out": w_out[l].astype(BF16),
        }
        ffn1 = (row(ffn1_norm[l]), ffn1_w1[l].astype(BF16), ffn1_w3[l].astype(BF16),
                ffn1_w2[l].astype(BF16))
        ffn2 = (row(ffn2_norm[l]), ffn2_w1[l].astype(BF16), ffn2_w3[l].astype(BF16),
                ffn2_w2[l].astype(BF16))

        lead = _ffn(lead, *ffn1)
        lead, state = _mixer(lead.reshape(1, LEAD, D_MODEL), mix, consts, n_pad=N_PAD, pos0=0,
                             emit_state=True)
        lead = lead.reshape(LEAD, D_MODEL)
        if l + 1 < depth:
            lead = _ffn(lead, *ffn2)

        h2 = _ffn(h.reshape(bsz * seq, D_MODEL), *ffn1)
        h = _mixer(h2.reshape(bsz, seq, D_MODEL), mix, consts, n_pad=N_PAD, pos0=LEAD,
                   init_state=state)
        h2 = _ffn(h.reshape(bsz * seq, D_MODEL), *ffn2)
        h = h2.reshape(bsz, seq, D_MODEL)
    return h
```

```python
import functools

import jax
import jax.numpy as jnp
from jax import lax
from jax.experimental import pallas as pl
from jax.experimental.pallas import tpu as pltpu

D_MODEL = 1024
N_META = 16
BLOCK = 128
LEAD = BLOCK
N_PAD = LEAD - N_META
ATTN_HEADS = 8
ATTN_KV_HEADS = 2
HEAD_DIM = 64
ATTN_GROUP = ATTN_HEADS // ATTN_KV_HEADS
ATTN_WIDTH = ATTN_HEADS * HEAD_DIM
KV_WIDTH = ATTN_KV_HEADS * HEAD_DIM
CONV_WIDTH = 512
CONV_KERNEL = 31
MLSTM_HEADS = 4
MLSTM_WIDTH = 512
MLSTM_HEAD_DIM = MLSTM_WIDTH // MLSTM_HEADS
MLSTM_CHUNK = 64
QK_CONV_KERNEL = 4
D_FF = 2816
N_BRANCH = 3
RMS_EPS = 1e-6
LN_EPS = 1e-5
NEG = -1e30

V7X_LANES = 128
V7X_SUBLANES = 8

ORIG_OFF_AQ = 0
ORIG_OFF_AK = ORIG_OFF_AQ + ATTN_WIDTH
ORIG_OFF_AV = ORIG_OFF_AK + KV_WIDTH
ORIG_OFF_CA = ORIG_OFF_AV + KV_WIDTH
ORIG_OFF_IF = ORIG_OFF_CA + 2 * CONV_WIDTH + 4 * MLSTM_WIDTH
ORIG_OFF_ZG = ORIG_OFF_IF + 2 * MLSTM_HEADS

OFF_AK = 0
OFF_CA = OFF_AK + KV_WIDTH
OFF_CG = OFF_CA + CONV_WIDTH
OFF_MQ = OFF_CG + CONV_WIDTH
OFF_MK = OFF_MQ + MLSTM_WIDTH
OFF_MV = OFF_MK + MLSTM_WIDTH
OFF_MO = OFF_MV + MLSTM_WIDTH
OFF_ZG = OFF_MO + MLSTM_WIDTH
OFF_IF = OFF_ZG + N_BRANCH * D_MODEL
N_IN_PACKED = OFF_IF + V7X_LANES

CONV_TAIL = 32
CONV_SHIFT_PAD = CONV_TAIL - V7X_SUBLANES
QK_TAIL = 8

FFN_ROW_TILE = 512
MIXER_TIME_TILE = 512
FFN_VMEM_LIMIT = 56 * 1024 * 1024
MIXER_VMEM_LIMIT = 60 * 1024 * 1024

F32 = jnp.float32
BF16 = jnp.bfloat16

STATE_SHAPES = (
    ((BLOCK, KV_WIDTH), BF16),
    ((KV_WIDTH, BLOCK), BF16),
    ((CONV_TAIL, CONV_WIDTH), F32),
    ((QK_TAIL, 2 * MLSTM_WIDTH), F32),
    ((MLSTM_HEADS, MLSTM_HEAD_DIM, MLSTM_HEAD_DIM), F32),
    ((V7X_SUBLANES, MLSTM_HEAD_DIM), F32),
    ((V7X_SUBLANES, V7X_LANES), F32),
)
N_MIXER_PARAMS = 22


def _const_spec(shape):
    nd = len(shape)
    return pl.BlockSpec(shape, lambda *_: (0,) * nd, pipeline_mode=pl.Buffered(1))


def _dot(a, b):
    return jnp.dot(a, b, preferred_element_type=F32)


def _dot_nt(a, b):
    return lax.dot_general(a, b, (((1,), (1,)), ((), ())), preferred_element_type=F32)


def _dot_tn(a, b):
    return lax.dot_general(a, b, (((0,), (0,)), ((), ())), preferred_element_type=F32)


def _rms_rows(x, gain):
    ms = jnp.mean(x * x, axis=-1, keepdims=True)
    return x * lax.rsqrt(ms + RMS_EPS) * gain


def _seg_sumsq(x, ones_ref):
    x2 = x * x
    hi = x2.astype(BF16)
    lo = (x2 - hi.astype(F32)).astype(BF16)
    ones = ones_ref[...]
    return _dot(hi, ones) + _dot(lo, ones)


def _ffn_body(x_ref, g_ref, w1_ref, w3_ref, w2_ref, o_ref):
    x = x_ref[...]
    h = _rms_rows(x, g_ref[...]).astype(BF16)
    a = _dot(h, w1_ref[...])
    b = _dot(h, w3_ref[...])
    u = (a * jax.nn.sigmoid(a) * b).astype(BF16)
    o_ref[...] = x + 0.5 * _dot(u, w2_ref[...])


def _ffn(x2d, g, w1, w3, w2):
    rows = x2d.shape[0]
    tm = min(FFN_ROW_TILE, rows)
    assert rows % tm == 0
    return pl.pallas_call(
        _ffn_body,
        grid=(rows // tm,),
        in_specs=[
            pl.BlockSpec((tm, D_MODEL), lambda i: (i, 0)),
            _const_spec((1, D_MODEL)),
            _const_spec((D_MODEL, D_FF)),
            _const_spec((D_MODEL, D_FF)),
            _const_spec((D_FF, D_MODEL)),
        ],
        out_specs=pl.BlockSpec((tm, D_MODEL), lambda i: (i, 0)),
        out_shape=jax.ShapeDtypeStruct(x2d.shape, F32),
        compiler_params=pltpu.CompilerParams(
            dimension_semantics=("arbitrary",), vmem_limit_bytes=FFN_VMEM_LIMIT),
        name="ffn",
    )(x2d, g, w1, w3, w2)


def _mixer_body(*refs, tt, n_pad, pos0, has_init, emit_state):
    n_state = len(STATE_SHAPES)
    n_in = N_MIXER_PARAMS + (n_state if has_init else 0)
    (x_ref, g_ref, win_ref, wqt_ref, wvt_ref, gb_ref, qg_ref, kg_ref, sink_ref,
     woa_ref, cw_ref, cb_ref, lng_ref, lnb_ref, woc_ref, qkw_ref, qkb_ref, ifb_ref,
     wom_ref, wout_ref, onesk_ref, band_ref) = refs[:N_MIXER_PARAMS]
    init_refs = refs[N_MIXER_PARAMS:n_in]
    o_ref = refs[n_in]
    n_out = 1 + (n_state if emit_state else 0)
    final_refs = refs[n_in + 1:n_in + n_out]
    (kext, vext_t, uext, ush, qkext, qm_s, km_s, vm_s, ht_s, bcum_s, r_s, ya_s,
     h_s, zg_s, c_s, n_s, m_s) = refs[n_in + n_out:]

    t_idx = pl.program_id(1)
    nblk = tt // BLOCK
    nchunk = tt // MLSTM_CHUNK
    mask_rows = pos0 < n_pad

    @pl.when(t_idx == 0)
    def _init_state():
        if has_init:
            k0, v0, u0, qk0, c0, n0, m0 = init_refs
            kext[0:BLOCK, :] = k0[...]
            vext_t[:, 0:BLOCK] = v0[...]
            uext[0:CONV_TAIL, :] = u0[...]
            qkext[0:QK_TAIL, :] = qk0[...]
            c_s[...] = c0[...]
            n_s[...] = n0[...]
            m_s[...] = m0[...]
        else:
            kext[0:BLOCK, :] = jnp.zeros((BLOCK, KV_WIDTH), BF16)
            vext_t[:, 0:BLOCK] = jnp.zeros((KV_WIDTH, BLOCK), BF16)
            uext[0:CONV_TAIL, :] = jnp.zeros((CONV_TAIL, CONV_WIDTH), F32)
            qkext[0:QK_TAIL, :] = jnp.zeros((QK_TAIL, 2 * MLSTM_WIDTH), F32)
            c_s[...] = jnp.zeros(c_s.shape, F32)
            n_s[...] = jnp.zeros(n_s.shape, F32)
            m_s[...] = jnp.full(m_s.shape, NEG, F32)

    t0 = t_idx * tt
    if mask_rows:
        valid_col = (lax.broadcasted_iota(jnp.int32, (tt, 1), 0) + (pos0 + t0)) >= n_pad

    def mask_pad_rows(v, fill):
        return jnp.where(valid_col, v, fill) if mask_rows else v

    x = x_ref[0]
    h = _rms_rows(x, g_ref[...]).astype(BF16)
    h_s[...] = h

    def proj(lo, hi):
        return _dot(h, win_ref[:, lo:hi])

    qt = _dot_nt(wqt_ref[...], h)
    vt = _dot_nt(wvt_ref[...], h)
    k = proj(OFF_AK, OFF_CA)
    k = k * lax.rsqrt(_seg_sumsq(k, onesk_ref) * (1.0 / HEAD_DIM) + RMS_EPS) * kg_ref[...]
    kext[BLOCK:BLOCK + tt, :] = k.astype(BF16)
    vext_t[:, BLOCK:BLOCK + tt] = vt.astype(BF16)

    qn = []
    for a in range(ATTN_HEADS):
        qa = qt[a * HEAD_DIM:(a + 1) * HEAD_DIM, :]
        inv = lax.rsqrt(jnp.mean(qa * qa, axis=0, keepdims=True) + RMS_EPS) * (HEAD_DIM ** -0.5)
        qn.append(qa * inv)
    qgain = qg_ref[...]
    band = band_ref[...]
    zeros_q = jnp.zeros((HEAD_DIM, ATTN_GROUP * BLOCK), BF16)
    key_iota = lax.broadcasted_iota(jnp.int32, (2 * BLOCK, ATTN_GROUP * BLOCK), 0)
    pj_piece = (OFF_ZG - OFF_CA) // (nblk * ATTN_KV_HEADS)
    pj_pieces = []
    for n in range(nblk):
        r0 = n * BLOCK
        if pos0 + r0 - BLOCK >= n_pad:
            bias = band
        else:
            key_pos = key_iota + (pos0 + t0 + r0 - BLOCK)
            bias = jnp.where(key_pos >= n_pad, band, NEG)
        kcat = kext[r0:r0 + 2 * BLOCK, :]
        for hk in range(ATTN_KV_HEADS):
            qh = jnp.concatenate(
                [(qn[hk * ATTN_GROUP + g][:, r0:r0 + BLOCK] * qgain).astype(BF16)
                 for g in range(ATTN_GROUP)], axis=1)
            rhs = jnp.concatenate([qh, zeros_q] if hk == 0 else [zeros_q, qh], axis=0)
            st = _dot(kcat, rhs) + bias
            pj_pieces.append(proj(OFF_CA + len(pj_pieces) * pj_piece, OFF_CA + (len(pj_pieces) + 1) * pj_piece))
            sink = sink_ref[hk:hk + 1, :]
            m = jnp.maximum(jnp.max(st, axis=0, keepdims=True), sink)
            p = jnp.exp(st - m)
            den = jnp.sum(p, axis=0, keepdims=True) + jnp.exp(sink - m)
            vth = vext_t[hk * HEAD_DIM:(hk + 1) * HEAD_DIM, r0:r0 + 2 * BLOCK]
            ot = _dot(vth, p.astype(BF16)) / den
            for j in range(ATTN_GROUP // 2):
                pair = jnp.concatenate(
                    [ot[:, (2 * j) * BLOCK:(2 * j + 1) * BLOCK],
                     ot[:, (2 * j + 1) * BLOCK:(2 * j + 2) * BLOCK]], axis=0)
                c0 = (hk * ATTN_GROUP + 2 * j) * HEAD_DIM
                ya_s[r0:r0 + BLOCK, c0:c0 + 2 * HEAD_DIM] = pair.T.astype(BF16)
    kext[0:BLOCK, :] = kext[tt:tt + BLOCK, :]
    vext_t[:, 0:BLOCK] = vext_t[:, tt:tt + BLOCK]
    ya = _dot(ya_s[...], woa_ref[...])

    pj = jnp.concatenate(pj_pieces, axis=1)

    def projected(lo, hi):
        return pj[:, lo - OFF_CA:hi - OFF_CA]

    ca = projected(OFF_CA, OFF_CG)
    cg = projected(OFF_CG, OFF_MQ)
    uext[CONV_TAIL:CONV_TAIL + tt, :] = mask_pad_rows(ca * jax.nn.sigmoid(cg), 0.0)
    for s in range(1, V7X_SUBLANES):
        ush[s - 1] = uext[s:s + tt + CONV_SHIFT_PAD, :]
    base = CONV_TAIL - (CONV_KERNEL - 1)
    acc = jnp.zeros((tt, CONV_WIDTH), F32) + cb_ref[...]
    for j in range(CONV_KERNEL):
        off = base + j
        s, a = off % V7X_SUBLANES, (off // V7X_SUBLANES) * V7X_SUBLANES
        tap = uext[a:a + tt, :] if s == 0 else ush[s - 1, a:a + tt, :]
        acc = acc + tap * cw_ref[j:j + 1, :]
    uext[0:CONV_TAIL, :] = uext[tt:tt + CONV_TAIL, :]
    mu = jnp.mean(acc, axis=-1, keepdims=True)
    xc = acc - mu
    yn = xc * lax.rsqrt(jnp.mean(xc * xc, axis=-1, keepdims=True) + LN_EPS)
    yn = yn * lng_ref[...] + lnb_ref[...]
    yc = _dot((yn * jax.nn.sigmoid(yn)).astype(BF16), woc_ref[...])

    mqk = projected(OFF_MQ, OFF_MV)
    qkext[QK_TAIL:QK_TAIL + tt, :] = mask_pad_rows(mqk, 0.0)
    qbase = QK_TAIL - (QK_CONV_KERNEL - 1)
    qacc = jnp.zeros((tt, 2 * MLSTM_WIDTH), F32) + qkb_ref[...]
    for j in range(QK_CONV_KERNEL):
        qacc = qacc + qkext[qbase + j:qbase + j + tt, :] * qkw_ref[j:j + 1, :]
    qkext[0:QK_TAIL, :] = qkext[tt:tt + QK_TAIL, :]
    qk = qacc * jax.nn.sigmoid(qacc)
    qm_s[...] = qk[:, :MLSTM_WIDTH] * (MLSTM_HEAD_DIM ** -0.5)
    km_s[...] = qk[:, MLSTM_WIDTH:]
    vm_s[...] = projected(OFF_MV, OFF_MO)

    zif = proj(OFF_IF, N_IN_PACKED) + ifb_ref[...]
    log_i = mask_pad_rows(zif, NEG)
    log_f = pltpu.roll(jax.nn.log_sigmoid(zif), V7X_LANES - MLSTM_HEADS, axis=1)
    pos = lax.broadcasted_iota(jnp.int32, (tt, 1), 0) % MLSTM_CHUNK
    bcum = log_f
    d = 1
    while d < MLSTM_CHUNK:
        bcum = bcum + jnp.where(pos >= d, pltpu.roll(bcum, d, axis=0), 0.0)
        d *= 2
    bcum_s[...] = bcum
    r_s[...] = log_i - bcum

    tri = (lax.broadcasted_iota(jnp.int32, (MLSTM_CHUNK, MLSTM_CHUNK), 0) >=
           lax.broadcasted_iota(jnp.int32, (MLSTM_CHUNK, MLSTM_CHUNK), 1))

    zg_slab = N_BRANCH * D_MODEL // nchunk
    assert zg_slab % V7X_LANES == 0

    def chunk_body(c, carry):
        def gate_project():
            z0 = pl.multiple_of(c * zg_slab, V7X_LANES)
            zg_s[:, pl.ds(z0, zg_slab)] = _dot(h_s[...], win_ref[:, pl.ds(OFF_ZG + z0, zg_slab)])

        r0 = pl.multiple_of(c * MLSTM_CHUNK, MLSTM_CHUNK)
        rows = pl.ds(r0, MLSTM_CHUNK)
        bc = bcum_s[rows, :]
        rc = r_s[rows, :]
        rct = rc.T
        for hd in range(MLSTM_HEADS):
            lanes = slice(hd * MLSTM_HEAD_DIM, (hd + 1) * MLSTM_HEAD_DIM)
            qc = qm_s[rows, lanes]
            kc = km_s[rows, lanes]
            vc = vm_s[rows, lanes]
            bcol = bc[:, hd:hd + 1]
            rcol = rc[:, hd:hd + 1]
            rrow = rct[hd:hd + 1, :]
            b_end = bc[MLSTM_CHUNK - 1:MLSTM_CHUNK, hd:hd + 1]
            m_prev = m_s[hd:hd + 1, 0:1]
            n_prev = n_s[hd:hd + 1, :]
            c_prev = c_s[hd]

            log_d = jnp.where(tri, bcol + rrow, NEG)
            m_inter = bcol + m_prev
            m_t = jnp.maximum(m_inter, jnp.max(log_d, axis=-1, keepdims=True))
            w_inter = jnp.exp(m_inter - m_t)
            qb = qc.astype(BF16)
            kb = kc.astype(BF16)
            vb = vc.astype(BF16)
            scores = _dot_nt(qb, kb)
            carried = _dot(qb, c_prev.astype(BF16))
            if hd == 0:
                gate_project()
            s = scores * jnp.exp(log_d - m_t)
            num = _dot(s.astype(BF16), vb) + w_inter * carried
            den = (jnp.sum(s, axis=-1, keepdims=True) +
                   w_inter * jnp.sum(qc * n_prev, axis=-1, keepdims=True))
            ht_s[rows, lanes] = num / jnp.maximum(jnp.abs(den), jnp.exp(-m_t))

            log_e = b_end + rcol
            m_loc = jnp.max(log_e, axis=0, keepdims=True)
            w_e = jnp.exp(log_e - m_loc)
            m_new = jnp.maximum(b_end + m_prev, m_loc)
            a = jnp.exp(b_end + m_prev - m_new)
            cc = jnp.exp(m_loc - m_new)
            kw = kc * w_e
            c_s[hd] = a * c_prev + cc * _dot_tn(kw.astype(BF16), vb)
            n_s[hd:hd + 1, :] = a * n_prev + cc * jnp.sum(kw, axis=0, keepdims=True)
            m_s[hd:hd + 1, :] = jnp.broadcast_to(m_new, (1, V7X_LANES))
        return carry

    lax.fori_loop(0, nchunk, chunk_body, 0)

    mo = projected(OFF_MO, OFF_ZG)
    ym = _dot(((0.5 + 0.5 * jnp.tanh(0.5 * mo)) * ht_s[...]).astype(BF16), wom_ref[...])

    def gate(i, y):
        zg = zg_s[:, i * D_MODEL:(i + 1) * D_MODEL] + gb_ref[i:i + 1, :]
        return (1.0 + jnp.tanh(0.5 * zg)) * y

    y = 0.5 * (gate(0, ya) + gate(1, yc) + gate(2, ym))
    o_ref[0] = x + _dot(y.astype(BF16), wout_ref[...])

    if emit_state:
        @pl.when(t_idx == pl.num_programs(1) - 1)
        def _emit_state():
            k1, v1, u1, qk1, c1, n1, m1 = final_refs
            k1[...] = kext[0:BLOCK, :]
            v1[...] = vext_t[:, 0:BLOCK]
            u1[...] = uext[0:CONV_TAIL, :]
            qk1[...] = qkext[0:QK_TAIL, :]
            c1[...] = c_s[...]
            n1[...] = n_s[...]
            m1[...] = m_s[...]


def _mixer(hseq, p, consts, *, n_pad, pos0, init_state=None, emit_state=False):
    bsz, t_len, _ = hseq.shape
    tt = min(MIXER_TIME_TILE, t_len)
    assert t_len % tt == 0 and tt % BLOCK == 0 and tt % MLSTM_CHUNK == 0
    assert tt >= CONV_TAIL and tt >= BLOCK
    assert not emit_state or bsz == 1
    has_init = init_state is not None
    nchunk = tt // MLSTM_CHUNK
    body = functools.partial(_mixer_body, tt=tt, n_pad=n_pad, pos0=pos0, has_init=has_init,
                             emit_state=emit_state)
    in_specs = [
        pl.BlockSpec((1, tt, D_MODEL), lambda b, t: (b, t, 0)),
        _const_spec((1, D_MODEL)),
        _const_spec((D_MODEL, N_IN_PACKED)),
        _const_spec((ATTN_WIDTH, D_MODEL)),
        _const_spec((KV_WIDTH, D_MODEL)),
        _const_spec((N_BRANCH, D_MODEL)),
        _const_spec((HEAD_DIM, BLOCK)),
        _const_spec((1, KV_WIDTH)),
        _const_spec((ATTN_KV_HEADS, ATTN_GROUP * BLOCK)),
        _const_spec((ATTN_WIDTH, D_MODEL)),
        _const_spec((CONV_KERNEL, CONV_WIDTH)),
        _const_spec((1, CONV_WIDTH)),
        _const_spec((1, CONV_WIDTH)),
        _const_spec((1, CONV_WIDTH)),
        _const_spec((CONV_WIDTH, D_MODEL)),
        _const_spec((QK_CONV_KERNEL, 2 * MLSTM_WIDTH)),
        _const_spec((1, 2 * MLSTM_WIDTH)),
        _const_spec((1, V7X_LANES)),
        _const_spec((MLSTM_WIDTH, D_MODEL)),
        _const_spec((D_MODEL, D_MODEL)),
        _const_spec((KV_WIDTH, KV_WIDTH)),
        _const_spec((2 * BLOCK, ATTN_GROUP * BLOCK)),
    ]
    args = [hseq, p["norm"], p["w_in"], p["w_q_t"], p["w_v_t"], p["gate_bias"], p["q_gain"],
            p["k_gain"], p["sinks"], p["w_o_attn"], p["conv_w"], p["conv_b"], p["ln_g"], p["ln_b"],
            p["w_o_conv"], p["qk_w"], p["qk_b"], p["if_bias"], p["w_o_mlstm"], p["w_out"],
            consts["ones_k"], consts["band"]]
    assert len(in_specs) == N_MIXER_PARAMS and len(args) == N_MIXER_PARAMS
    if has_init:
        in_specs += [_const_spec(shape) for shape, _ in STATE_SHAPES]
        args += list(init_state)
    out_specs = [pl.BlockSpec((1, tt, D_MODEL), lambda b, t: (b, t, 0))]
    out_shape = [jax.ShapeDtypeStruct(hseq.shape, F32)]
    if emit_state:
        for shape, dtype in STATE_SHAPES:
            out_specs.append(pl.BlockSpec(shape, lambda b, t, nd=len(shape): (0,) * nd))
            out_shape.append(jax.ShapeDtypeStruct(shape, dtype))
    scratch = [
        pltpu.VMEM((BLOCK + tt, KV_WIDTH), BF16),
        pltpu.VMEM((KV_WIDTH, BLOCK + tt), BF16),
        pltpu.VMEM((CONV_TAIL + tt, CONV_WIDTH), F32),
        pltpu.VMEM((V7X_SUBLANES - 1, tt + CONV_SHIFT_PAD, CONV_WIDTH), F32),
        pltpu.VMEM((QK_TAIL + tt, 2 * MLSTM_WIDTH), F32),
        pltpu.VMEM((tt, MLSTM_WIDTH), F32),
        pltpu.VMEM((tt, MLSTM_WIDTH), F32),
        pltpu.VMEM((tt, MLSTM_WIDTH), F32),
        pltpu.VMEM((tt, MLSTM_WIDTH), F32),
        pltpu.VMEM((tt, V7X_LANES), F32),
        pltpu.VMEM((tt, V7X_LANES), F32),
        pltpu.VMEM((tt, ATTN_WIDTH), BF16),
        pltpu.VMEM((tt, D_MODEL), BF16),
        pltpu.VMEM((tt, N_BRANCH * D_MODEL), F32),
        pltpu.VMEM((MLSTM_HEADS, MLSTM_HEAD_DIM, MLSTM_HEAD_DIM), F32),
        pltpu.VMEM((V7X_SUBLANES, MLSTM_HEAD_DIM), F32),
        pltpu.VMEM((V7X_SUBLANES, V7X_LANES), F32),
    ]
    outs = pl.pallas_call(
        body,
        grid=(bsz, t_len // tt),
        in_specs=in_specs,
        out_specs=out_specs,
        out_shape=out_shape,
        scratch_shapes=scratch,
        compiler_params=pltpu.CompilerParams(
            dimension_semantics=("arbitrary", "arbitrary"),
            vmem_limit_bytes=MIXER_VMEM_LIMIT),
        name="mixer_lead" if emit_state else "mixer",
    )(*args)
    return (outs[0], tuple(outs[1:])) if emit_state else outs[0]


def _make_consts():
    i = jnp.arange(KV_WIDTH) // HEAD_DIM
    kj = jnp.arange(2 * BLOCK)
    qi = jnp.arange(ATTN_GROUP * BLOCK) % BLOCK
    allowed = (kj[:, None] > qi[None, :]) & (kj[:, None] <= qi[None, :] + BLOCK)
    return {
        "ones_k": (i[:, None] == i[None, :]).astype(BF16),
        "band": jnp.where(allowed, 0.0, NEG).astype(F32),
    }


def _pack_w_in(w_in):
    pad = jnp.zeros((D_MODEL, V7X_LANES - 2 * MLSTM_HEADS), w_in.dtype)
    return jnp.concatenate(
        [w_in[:, ORIG_OFF_AK:ORIG_OFF_AV], w_in[:, ORIG_OFF_CA:ORIG_OFF_IF], w_in[:, ORIG_OFF_ZG:],
         w_in[:, ORIG_OFF_IF:ORIG_OFF_ZG], pad], axis=1).astype(BF16)


def kernel(x, meta_tokens, ffn1_norm, ffn1_w1, ffn1_w3, ffn1_w2, mix_norm, w_in, gate_bias,
           attn_q_norm, attn_k_norm, attn_sinks, w_o_attn, conv_dw_w, conv_dw_b, conv_ln_g,
           conv_ln_b, w_o_conv, mlstm_qk_conv_w, mlstm_qk_conv_b, mlstm_igate_bias,
           mlstm_fgate_bias, w_o_mlstm, w_out, ffn2_norm, ffn2_w1, ffn2_w3, ffn2_w2):
    bsz, seq, _ = x.shape
    depth = w_in.shape[0]
    consts = _make_consts()

    def row(v):
        return v.reshape(1, -1).astype(F32)

    lead = jnp.concatenate(
        [jnp.zeros((N_PAD, D_MODEL), x.dtype), meta_tokens.astype(x.dtype)], axis=0)
    h = x

    for l in range(depth):
        if_bias = jnp.concatenate([
            mlstm_igate_bias[l], mlstm_fgate_bias[l],
            jnp.zeros((V7X_LANES - 2 * MLSTM_HEADS,), F32)]).reshape(1, V7X_LANES)
        mix = {
            "norm": row(mix_norm[l]),
            "w_in": _pack_w_in(w_in[l]),
            "w_q_t": w_in[l][:, ORIG_OFF_AQ:ORIG_OFF_AK].T.astype(BF16),
            "w_v_t": w_in[l][:, ORIG_OFF_AV:ORIG_OFF_CA].T.astype(BF16),
            "gate_bias": gate_bias[l].astype(F32),
            "q_gain": jnp.broadcast_to(attn_q_norm[l].astype(F32)[:, None], (HEAD_DIM, BLOCK)),
            "k_gain": row(jnp.tile(attn_k_norm[l], ATTN_KV_HEADS)),
            "sinks": jnp.repeat(attn_sinks[l].astype(F32), BLOCK).reshape(
                ATTN_KV_HEADS, ATTN_GROUP * BLOCK),
            "w_o_attn": w_o_attn[l].astype(BF16),
            "conv_w": conv_dw_w[l].astype(F32),
            "conv_b": row(conv_dw_b[l]),
            "ln_g": row(conv_ln_g[l]),
            "ln_b": row(conv_ln_b[l]),
            "w_o_conv": w_o_conv[l].astype(BF16),
            "qk_w": mlstm_qk_conv_w[l].astype(F32),
            "qk_b": row(mlstm_qk_conv_b[l]),
            "if_bias": if_bias,
            "w_o_mlstm": w_o_mlstm[l].astype(BF16),
            "w_out": w_out[l].astype(BF16),
        }
        ffn1 = (row(ffn1_norm[l]), ffn1_w1[l].astype(BF16), ffn1_w3[l].astype(BF16),
                ffn1_w2[l].astype(BF16))
        ffn2 = (row(ffn2_norm[l]), ffn2_w1[l].astype(BF16), ffn2_w3[l].astype(BF16),
                ffn2_w2[l].astype(BF16))

        lead = _ffn(lead, *ffn1)
        lead, state = _mixer(lead.reshape(1, LEAD, D_MODEL), mix, consts, n_pad=N_PAD, pos0=0,
                             emit_state=True)
        lead = lead.reshape(LEAD, D_MODEL)
        if l + 1 < depth:
            lead = _ffn(lead, *ffn2)

        h2 = _ffn(h.reshape(bsz * seq, D_MODEL), *ffn1)
        h = _mixer(h2.reshape(bsz, seq, D_MODEL), mix, consts, n_pad=N_PAD, pos0=LEAD,
                   init_state=state)
        h2 = _ffn(h.reshape(bsz * seq, D_MODEL), *ffn2)
        h = h2.reshape(bsz, seq, D_MODEL)
    return h
```
